```python
import math
import jax, jax.numpy as jnp
from jax import lax
import numpy as np


D_MODEL = 1024
BATCH = 8
SEQ = 4096
DEPTH = 4

GRID_W = 64
CTX_LEN = 256
HEAD_DIM = D_MODEL // 16
MLSTM_HEADS = 4
MLSTM_W = MLSTM_HEADS * HEAD_DIM
DIFF_HEADS = 4
DIFF_DV = 2 * HEAD_DIM
DIFF_W = DIFF_HEADS * DIFF_DV
GQA_HEADS = 4
GQA_KV_HEADS = 2
GQA_W = GQA_HEADS * HEAD_DIM
D_MIX = MLSTM_W + DIFF_W + GQA_W
MLSTM_CONV_W = 5
MLSTM_CHUNK = 64
Q_BLOCK = 128
ROPE_THETA = 10000.0
D_FF = (8 * D_MODEL + 3 * 256 - 1) // (3 * 256) * 256
ALPHA = (2 * DEPTH) ** 0.25
BETA = (8 * DEPTH) ** -0.25
LN_EPS = 1e-5
SPLIT_SIZES = (MLSTM_W, MLSTM_W, MLSTM_W, MLSTM_W, 4 * MLSTM_HEADS,
               2 * DIFF_HEADS * HEAD_DIM, 2 * DIFF_HEADS * HEAD_DIM, DIFF_W,
               GQA_W, GQA_KV_HEADS * HEAD_DIM, GQA_KV_HEADS * HEAD_DIM)
IN_COLS = sum(SPLIT_SIZES)

kernel_name = 'hybrid_mlstm_diffattn_gqa_dit_block'


def _layer_norm(x, g, b):
    xf = x.astype(jnp.float32)
    mu = xf.mean(-1, keepdims=True)
    var = jnp.square(xf - mu).mean(-1, keepdims=True)
    return ((xf - mu) * lax.rsqrt(var + LN_EPS) * g + b).astype(x.dtype)


def _rms_norm(x, g):
    xf = x.astype(jnp.float32)
    return (xf * lax.rsqrt(jnp.mean(xf * xf, -1, keepdims=True) + LN_EPS) * g).astype(x.dtype)


def _rope_tables(rows):
    row = jnp.repeat(jnp.arange(rows, dtype=jnp.float32), GRID_W)
    col = jnp.tile(jnp.arange(GRID_W, dtype=jnp.float32), rows)
    n_freq = HEAD_DIM // 4
    inv = ROPE_THETA ** (-jnp.arange(n_freq, dtype=jnp.float32) / n_freq)
    ar = row[:, None] * inv
    ac = col[:, None] * inv
    ang = jnp.concatenate([ar, ar, ac, ac], axis=-1)
    return jnp.cos(ang), jnp.sin(ang)


def _rot_half(u):
    u1, u2 = jnp.split(u, 2, axis=-1)
    return jnp.concatenate([-u2, u1], axis=-1)


def _apply_rope(x, cos, sin):
    xf = x.astype(jnp.float32)
    xr, xc = jnp.split(xf, 2, axis=-1)
    rot = jnp.concatenate([_rot_half(xr), _rot_half(xc)], axis=-1)
    return (xf * cos + rot * sin).astype(x.dtype)


def _heads(a, n_heads):
    b, s, _ = a.shape
    return a.reshape(b, s, n_heads, -1).transpose(0, 2, 1, 3)


def _merge_heads(a):
    b, h, s, d = a.shape
    return a.transpose(0, 2, 1, 3).reshape(b, s, h * d)


def _split_cols(p):
    out, start = [], 0
    for size in SPLIT_SIZES:
        out.append(p[..., start:start + size])
        start += size
    return out


def _dw_conv(u, w, b):
    k, ch = w.shape
    y = lax.conv_general_dilated(u, w[:, None, :], window_strides=(1,),
                                 padding=((k // 2, k // 2),),
                                 dimension_numbers=('NWC', 'WIO', 'NWC'),
                                 feature_group_count=ch)
    return y + b


def _to_blocks(a):
    *lead, s, d = a.shape
    return jnp.moveaxis(a.reshape(*lead, s // Q_BLOCK, Q_BLOCK, d), -3, 0)


def _from_blocks(a):
    a = jnp.moveaxis(a, 0, -3)
    *lead, nb, bq, d = a.shape
    return a.reshape(*lead, nb * bq, d)


def _diff_attention(q1, q2, k1, k2, v, lam):
    scale = HEAD_DIM ** -0.5

    def block(qs):
        qa, qb = qs
        p1 = jax.nn.softmax((jnp.einsum('bhqd,bhkd->bhqk', qa, k1) * scale).astype(jnp.float32), axis=-1)
        p2 = jax.nn.softmax((jnp.einsum('bhqd,bhkd->bhqk', qb, k2) * scale).astype(jnp.float32), axis=-1)
        return jnp.einsum('bhqk,bhkv->bhqv', (p1 - lam * p2).astype(v.dtype), v)

    return _from_blocks(lax.map(block, (_to_blocks(q1), _to_blocks(q2))))


def _gqa_attention(q, k, v):
    b, hq, s, d = q.shape
    qg = q.reshape(b, GQA_KV_HEADS, hq // GQA_KV_HEADS, s, d)
    scale = d ** -0.5

    def block(qb):
        p = jax.nn.softmax((jnp.einsum('bhgqd,bhkd->bhgqk', qb, k) * scale).astype(jnp.float32), axis=-1)
        return jnp.einsum('bhgqk,bhkd->bhgqd', p.astype(v.dtype), v)

    return _from_blocks(lax.map(block, _to_blocks(qg))).reshape(b, hq, s, d)


def _mlstm_scan(q, k, v, i_pre, f_pre, state):
    b, h, s, d = q.shape
    nc = s // MLSTM_CHUNK

    def chunks(a):
        return jnp.moveaxis(a.reshape(b, h, nc, MLSTM_CHUNK, *a.shape[3:]), 2, 0)

    xs = (chunks(q.astype(jnp.float32) * d ** -0.5), chunks(k.astype(jnp.float32)),
          chunks(v.astype(jnp.float32)), chunks(i_pre), chunks(jax.nn.log_sigmoid(f_pre)))
    lower = jnp.tril(jnp.ones((MLSTM_CHUNK, MLSTM_CHUNK), dtype=bool))

    def step(carry, inp):
        c_mat, n_vec, m = carry
        qc, kc, vc, ic, lf = inp
        bcum = jnp.cumsum(lf, axis=-1)
        logw = jnp.where(lower, bcum[..., :, None] - bcum[..., None, :] + ic[..., None, :], -jnp.inf)
        inter = bcum + m[..., None]
        m_t = jnp.maximum(logw.max(-1), inter)
        w = jnp.exp(logw - m_t[..., None])
        w_prev = jnp.exp(inter - m_t)
        sw = jnp.einsum('bhtd,bhsd->bhts', qc, kc) * w
        num = jnp.einsum('bhts,bhsv->bhtv', sw, vc) + w_prev[..., None] * jnp.einsum('bhtd,bhdv->bhtv', qc, c_mat)
        den = sw.sum(-1) + w_prev * jnp.einsum('bhtd,bhd->bht', qc, n_vec)
        h_out = num / jnp.maximum(jnp.abs(den), jnp.exp(-m_t))[..., None]
        b_last = bcum[..., -1]
        logu = b_last[..., None] - bcum + ic
        m_new = jnp.maximum(b_last + m, logu.max(-1))
        decay = jnp.exp(b_last + m - m_new)
        u = jnp.exp(logu - m_new[..., None])
        c_new = decay[..., None, None] * c_mat + jnp.einsum('bhs,bhsd,bhsv->bhdv', u, kc, vc)
        n_new = decay[..., None] * n_vec + jnp.einsum('bhs,bhsd->bhd', u, kc)
        return (c_new, n_new, m_new), h_out

    state, hs = lax.scan(step, state, xs)
    return jnp.moveaxis(hs, 0, 2).reshape(b, h, s, d), state


def _mlstm_prep(mq, mk, mv, mg, conv_w, conv_b, gate_b):
    qk = jax.nn.silu(_dw_conv(jnp.concatenate([mq, mk], axis=-1), conv_w, conv_b))
    q = _heads(qk[..., :MLSTM_W], MLSTM_HEADS)
    k = _heads(qk[..., MLSTM_W:], MLSTM_HEADS)
    v = _heads(mv, MLSTM_HEADS)
    g = (mg + gate_b).astype(jnp.float32).transpose(0, 2, 1)
    i_f, i_b, f_f, f_b = jnp.split(g, 4, axis=1)
    return q, k, v, (i_f, f_f), (i_b, f_b)


def _mlstm_bidirectional(lat, ctx):
    ql, kl, vl, fwd_l, bwd_l = lat
    qc, kc, vc, fwd_c, bwd_c = ctx
    b = ql.shape[0]
    zero = (jnp.zeros((b, MLSTM_HEADS, HEAD_DIM, HEAD_DIM), jnp.float32),
            jnp.zeros((b, MLSTM_HEADS, HEAD_DIM), jnp.float32),
            jnp.zeros((b, MLSTM_HEADS), jnp.float32))
    hcf, st_f = _mlstm_scan(qc, kc, vc, fwd_c[0], fwd_c[1], zero)
    hlf, _ = _mlstm_scan(ql, kl, vl, fwd_l[0], fwd_l[1], st_f)
    rev = lambda a: jnp.flip(a, axis=2)
    hcb, st_b = _mlstm_scan(rev(qc), rev(kc), rev(vc), rev(bwd_c[0]), rev(bwd_c[1]), zero)
    hlb, _ = _mlstm_scan(rev(ql), rev(kl), rev(vl), rev(bwd_l[0]), rev(bwd_l[1]), st_b)
    return hlf + rev(hlb), hcf + rev(hcb)


def _mlstm_out(h, o, g):
    mu = h.mean(-1, keepdims=True)
    var = jnp.square(h - mu).mean(-1, keepdims=True)
    hn = (h - mu) * lax.rsqrt(var + LN_EPS)
    return (_merge_heads(hn) * g * jax.nn.sigmoid(o.astype(jnp.float32))).astype(o.dtype)


def _diff_split(a):
    b, s, _ = a.shape
    a = a.reshape(b, s, DIFF_HEADS, 2, HEAD_DIM).transpose(0, 2, 3, 1, 4)
    return a[:, :, 0], a[:, :, 1]


def _diff_out(o, g, lam_init):
    return _merge_heads(_rms_norm(o, g) * (1.0 - lam_init))


def _hybrid_mixer(p_lat, p_ctx, conv_w, conv_b, gate_b, mnorm_g, lam_vecs, lam_init,
                  dnorm_g, qn_g, kn_g, cos, sin, ctx_out):
    mq, mk, mv, mo, mg, dq, dk, dv, gq, gk, gv = _split_cols(p_lat)
    mqc, mkc, mvc, moc, mgc, dqc, dkc, dvc, gqc, gkc, gvc = _split_cols(p_ctx)
    rope = lambda a: _apply_rope(a, cos, sin)
    h_lat, h_ctx = _mlstm_bidirectional(_mlstm_prep(mq, mk, mv, mg, conv_w, conv_b, gate_b),
                                        _mlstm_prep(mqc, mkc, mvc, mgc, conv_w, conv_b, gate_b))
    a_lat = _mlstm_out(h_lat, mo, mnorm_g)
    lv = lam_vecs.astype(jnp.float32)
    lam = jnp.exp(jnp.sum(lv[0] * lv[1])) - jnp.exp(jnp.sum(lv[2] * lv[3])) + lam_init
    q1, q2 = _diff_split(dq)
    k1, k2 = _diff_split(dk)
    k1c, k2c = _diff_split(dkc)
    vd, vdc = _heads(dv, DIFF_HEADS), _heads(dvc, DIFF_HEADS)
    k1_all = jnp.concatenate([k1c, rope(k1)], axis=2)
    k2_all = jnp.concatenate([k2c, rope(k2)], axis=2)
    vd_all = jnp.concatenate([vdc, vd], axis=2)
    b_lat = _diff_out(_diff_attention(rope(q1), rope(q2), k1_all, k2_all, vd_all, lam), dnorm_g, lam_init)
    qg = rope(_rms_norm(_heads(gq, GQA_HEADS), qn_g))
    kg = rope(_rms_norm(_heads(gk, GQA_KV_HEADS), kn_g))
    kgc = _rms_norm(_heads(gkc, GQA_KV_HEADS), kn_g)
    vg, vgc = _heads(gv, GQA_KV_HEADS), _heads(gvc, GQA_KV_HEADS)
    c_lat = _merge_heads(_gqa_attention(qg, jnp.concatenate([kgc, kg], axis=2),
                                        jnp.concatenate([vgc, vg], axis=2)))
    y_lat = jnp.concatenate([a_lat, b_lat, c_lat], axis=-1)
    if not ctx_out:
        return y_lat, None
    q1c, q2c = _diff_split(dqc)
    qgc = _rms_norm(_heads(gqc, GQA_HEADS), qn_g)
    a_ctx = _mlstm_out(h_ctx, moc, mnorm_g)
    b_ctx = _diff_out(_diff_attention(q1c, q2c, k1c, k2c, vdc, lam), dnorm_g, lam_init)
    cm_ctx = _merge_heads(_gqa_attention(qgc, kgc, vgc))
    return y_lat, jnp.concatenate([a_ctx, b_ctx, cm_ctx], axis=-1)


def _swiglu(u, w_in, w_out):
    gate, up = jnp.split(u @ w_in, 2, axis=-1)
    return (jax.nn.silu(gate) * up) @ w_out


def setup_inputs(seed: int = 0) -> dict:
    key = jax.random.key(seed)
    ks = jax.random.split(key, 24)
    nrm = lambda k, shape, s: s * jax.random.normal(k, shape, jnp.float32)
    gate_i = nrm(ks[9], (DEPTH, 2 * MLSTM_HEADS), 0.1)
    gate_f = jnp.tile(jnp.linspace(3.0, 6.0, MLSTM_HEADS, dtype=jnp.float32), 2) + nrm(ks[10], (DEPTH, 2 * MLSTM_HEADS), 0.1)
    return {
        'x': nrm(ks[0], (BATCH, SEQ, D_MODEL), 1.0),
        'c': nrm(ks[1], (BATCH, D_MODEL), 1.0),
        'ctx': nrm(ks[2], (BATCH, CTX_LEN, D_MODEL), 1.0),
        'c_ctx': nrm(ks[3], (D_MODEL,), 1.0),
        'w_ada': nrm(ks[4], (DEPTH, D_MODEL, 6 * D_MODEL), 0.5 * D_MODEL ** -0.5),
        'b_ada': nrm(ks[5], (DEPTH, 6 * D_MODEL), 0.01),
        'w_in': nrm(ks[6], (DEPTH, D_MODEL, IN_COLS), D_MODEL ** -0.5),
        'mlstm_conv_w': nrm(ks[7], (DEPTH, MLSTM_CONV_W, 2 * MLSTM_W), MLSTM_CONV_W ** -0.5),
        'mlstm_conv_b': nrm(ks[8], (DEPTH, 2 * MLSTM_W), 0.01),
        'mlstm_gate_b': jnp.concatenate([gate_i, gate_f], axis=-1),
        'mlstm_norm_g': 1.0 + nrm(ks[11], (DEPTH, MLSTM_W), 0.1),
        'diff_lambda': nrm(ks[12], (DEPTH, 4, HEAD_DIM), 0.1),
        'diff_norm_g': 1.0 + nrm(ks[13], (DEPTH, DIFF_DV), 0.1),
        'gqa_q_norm_g': 1.0 + nrm(ks[14], (DEPTH, HEAD_DIM), 0.1),
        'gqa_k_norm_g': 1.0 + nrm(ks[15], (DEPTH, HEAD_DIM), 0.1),
        'w_out': nrm(ks[16], (DEPTH, D_MIX, D_MODEL), BETA * D_MIX ** -0.5),
        'ln1_g': 1.0 + nrm(ks[17], (DEPTH, D_MODEL), 0.1),
        'ln1_b': nrm(ks[18], (DEPTH, D_MODEL), 0.01),
        'w_ffn_in': nrm(ks[19], (DEPTH, D_MODEL, 2 * D_FF), D_MODEL ** -0.5),
        'w_ffn_out': nrm(ks[20], (DEPTH, D_FF, D_MODEL), BETA * D_FF ** -0.5),
        'ln2_g': 1.0 + nrm(ks[21], (DEPTH, D_MODEL), 0.1),
        'ln2_b': nrm(ks[22], (DEPTH, D_MODEL), 0.01),
    }


def reference(x, c, ctx, c_ctx, w_ada, b_ada, w_in, mlstm_conv_w, mlstm_conv_b, mlstm_gate_b,
              mlstm_norm_g, diff_lambda, diff_norm_g, gqa_q_norm_g, gqa_k_norm_g, w_out,
              ln1_g, ln1_b, w_ffn_in, w_ffn_out, ln2_g, ln2_b):
    rows = x.shape[1] // GRID_W
    cos, sin = _rope_tables(rows)
    xc = ctx
    for l in range(DEPTH):
        last = l == DEPTH - 1
        lam_init = 0.8 - 0.6 * math.exp(-0.3 * l)
        mod = jax.nn.silu(c) @ w_ada[l] + b_ada[l]
        sh1, sc1, g1, sh2, sc2, g2 = jnp.split(mod[:, None, :], 6, axis=-1)
        modc = jax.nn.silu(c_ctx) @ w_ada[l] + b_ada[l]
        sh1c, sc1c, g1c, sh2c, sc2c, g2c = jnp.split(modc, 6)
        y, yc = _hybrid_mixer((x * (1 + sc1) + sh1) @ w_in[l], (xc * (1 + sc1c) + sh1c) @ w_in[l],
                              mlstm_conv_w[l], mlstm_conv_b[l], mlstm_gate_b[l], mlstm_norm_g[l],
                              diff_lambda[l], lam_init, diff_norm_g[l], gqa_q_norm_g[l], gqa_k_norm_g[l],
                              cos, sin, not last)
        x = _layer_norm(ALPHA * x + g1 * (y @ w_out[l]), ln1_g[l], ln1_b[l])
        x = _layer_norm(ALPHA * x + g2 * _swiglu(x * (1 + sc2) + sh2, w_ffn_in[l], w_ffn_out[l]), ln2_g[l], ln2_b[l])
        if not last:
            xc = _layer_norm(ALPHA * xc + g1c * (yc @ w_out[l]), ln1_g[l], ln1_b[l])
            xc = _layer_norm(ALPHA * xc + g2c * _swiglu(xc * (1 + sc2c) + sh2c, w_ffn_in[l], w_ffn_out[l]), ln2_g[l], ln2_b[l])
    return x
```

```python
import functools
import math

import jax
import jax.numpy as jnp
from jax import lax
from jax.experimental import pallas as pl
from jax.experimental.pallas import tpu as pltpu

F32 = jnp.float32
BF16 = jnp.bfloat16

HEAD_DIM = 64
MLSTM_HEADS = 4
DIFF_HEADS = 4
GQA_HEADS = 4
GQA_KV_HEADS = 2
GRID_W = 64
CONV_W = 5
ROPE_THETA = 10000.0
LN_EPS = 1e-5

MLSTM_W = MLSTM_HEADS * HEAD_DIM
DIFF_W = DIFF_HEADS * 2 * HEAD_DIM
GQA_W = GQA_HEADS * HEAD_DIM
GQA_KV_W = GQA_KV_HEADS * HEAD_DIM

LANES = 128
BF16_SUBLANES = 16
VMEM_LIMIT_BYTES = 56 * 1024 * 1024

ROW_TILE = 256
ROPE_HALF = HEAD_DIM // 4

C_MQK, C_MV, C_MO = 0, 512, 768
C_DQ, C_DK, C_DV = 1024, 1536, 2048
C_GQ, C_GK, C_GV, C_GATE = 2560, 2816, 2944, 3200
IN_COLS_PADDED = 3328


def _cparams(*sem):
    return pltpu.CompilerParams(dimension_semantics=sem, vmem_limit_bytes=VMEM_LIMIT_BYTES)


def _dot(a, b):
    return jnp.dot(a, b, preferred_element_type=F32)


def _dot_nt(a, b):
    return lax.dot_general(a, b, (((1,), (1,)), ((), ())), preferred_element_type=F32)


def _dot_tn(a, b):
    return lax.dot_general(a, b, (((0,), (0,)), ((), ())), preferred_element_type=F32)


def _split_bf16(a):
    hi = a.astype(BF16)
    lo = (a - hi.astype(F32)).astype(BF16)
    return hi, lo


def _dot_split_lhs(a, m):
    hi, lo = _split_bf16(a)
    return _dot(hi, m) + _dot(lo, m)


def _dot_split_rhs(m, a):
    hi, lo = _split_bf16(a)
    return _dot(m, hi) + _dot(m, lo)


def _head_mean_matrix(width):
    r = lax.broadcasted_iota(jnp.int32, (width, width), 0) // HEAD_DIM
    c = lax.broadcasted_iota(jnp.int32, (width, width), 1) // HEAD_DIM
    return jnp.where(r == c, 1.0 / HEAD_DIM, 0.0).astype(BF16)


def _layer_norm(z, g, b):
    mu = jnp.mean(z, axis=-1, keepdims=True)
    zc = z - mu
    var = jnp.mean(zc * zc, axis=-1, keepdims=True)
    return zc * lax.rsqrt(var + LN_EPS) * g + b


def _ada_kernel(c_ref, w_ref, b_ref, o_ref):
    c = c_ref[...]
    o_ref[0] = _dot(c * jax.nn.sigmoid(c), w_ref[0]) + b_ref[0]


def _ada_call(cvec, w_ada, b_ada):
    depth, d, six_d = w_ada.shape
    rows = cvec.shape[0]
    bn = 1024
    return pl.pallas_call(
        _ada_kernel,
        grid=(depth, six_d // bn),
        in_specs=[pl.BlockSpec((rows, d), lambda l, j: (0, 0)),
                  pl.BlockSpec((1, d, bn), lambda l, j: (l, 0, j)),
                  pl.BlockSpec((1, 1, bn), lambda l, j: (l, 0, j))],
        out_specs=pl.BlockSpec((1, rows, bn), lambda l, j: (l, 0, j)),
        out_shape=jax.ShapeDtypeStruct((depth, rows, six_d), F32),
        compiler_params=_cparams("parallel", "parallel"),
        name="adaln_mod",
    )(cvec, w_ada, b_ada.reshape(depth, 1, six_d))


def _inproj_kernel(x_ref, mod_ref, w_ref, cos_ref, sa_ref, sb_ref, qg_ref, kg_ref,
                   mqk_ref, mv_ref, mo_ref, gate_ref, dq_ref, dk_ref, dv_ref,
                   gq_ref, gk_ref, gv_ref):
    x = x_ref[0]
    mod = mod_ref[0, 0]
    xm = (x * (1.0 + mod[1:2]) + mod[0:1]).astype(BF16)
    cos, sa, sb = cos_ref[...], sa_ref[...], sb_ref[...]
    scale = HEAD_DIM ** -0.5

    def proj(lo, n):
        return _dot(xm, w_ref[:, lo:lo + n])

    def rope(a):
        return (a * cos + pltpu.roll(a, LANES - ROPE_HALF, 1) * sa
                + pltpu.roll(a, ROPE_HALF, 1) * sb)

    mqk_ref[0] = proj(C_MQK, 2 * MLSTM_W).astype(BF16)
    mv_ref[0] = proj(C_MV, MLSTM_W).astype(BF16)
    mo_ref[0] = proj(C_MO, MLSTM_W).astype(BF16)
    gate_ref[0] = proj(C_GATE, LANES)

    acc = proj(C_DQ, DIFF_W)
    for j in range(DIFF_W // LANES):
        dq_ref[0, :, j * LANES:(j + 1) * LANES] = (
            rope(acc[:, j * LANES:(j + 1) * LANES]) * scale).astype(BF16)
    acc = proj(C_DK, DIFF_W)
    for j in range(DIFF_W // LANES):
        dk_ref[0, :, j * LANES:(j + 1) * LANES] = rope(acc[:, j * LANES:(j + 1) * LANES]).astype(BF16)
    dv_ref[0] = proj(C_DV, DIFF_W).astype(BF16)

    head_mean = _head_mean_matrix(LANES)

    def rms(a, g):
        ms = _dot_split_lhs(a * a, head_mean)
        return a * lax.rsqrt(ms + LN_EPS) * g

    acc = proj(C_GQ, GQA_W)
    qg = qg_ref[...]
    for j in range(GQA_W // LANES):
        gq_ref[0, :, j * LANES:(j + 1) * LANES] = (
            rope(rms(acc[:, j * LANES:(j + 1) * LANES], qg)) * scale).astype(BF16)
    gk_ref[0] = rope(rms(proj(C_GK, GQA_KV_W), kg_ref[...])).astype(BF16)

    lane = lax.broadcasted_iota(jnp.int32, (1, 2 * LANES), 1)
    ones_col = jnp.logical_or(lane == HEAD_DIM, lane == LANES).astype(F32)
    gv_ref[0] = (proj(C_GV, 2 * LANES) + ones_col).astype(BF16)


def _inproj_call(xs, mod, w, tabs, qg, kg, n_lat):
    b, t, d = xs.shape
    n_tot = t // ROW_TILE
    tm = ROW_TILE
    row = lambda width: pl.BlockSpec((1, tm, width), lambda bi, i: (bi, i, 0))
    tab = pl.BlockSpec((tm, LANES), lambda bi, i: (i, 0))
    vec = pl.BlockSpec((1, LANES), lambda bi, i: (0, 0))
    widths = (2 * MLSTM_W, MLSTM_W, MLSTM_W, LANES, DIFF_W, DIFF_W, DIFF_W, GQA_W, GQA_KV_W, 2 * LANES)
    dtypes = (BF16, BF16, BF16, F32, BF16, BF16, BF16, BF16, BF16, BF16)
    return pl.pallas_call(
        _inproj_kernel,
        grid=(b, n_tot),
        in_specs=[row(d),
                  pl.BlockSpec((1, 1, 6, d), lambda bi, i: (bi, jnp.minimum(i // n_lat, 1), 0, 0)),
                  pl.BlockSpec((d, IN_COLS_PADDED), lambda bi, i: (0, 0)),
                  tab, tab, tab, vec, vec],
        out_specs=[row(wd) for wd in widths],
        out_shape=[jax.ShapeDtypeStruct((b, t, wd), dt) for wd, dt in zip(widths, dtypes)],
        compiler_params=_cparams("parallel", "parallel"),
        name="in_proj",
    )(xs, mod, w, *tabs, qg, kg)


def _mlstm_kernel(mqk_ref, mv_ref, mo_ref, gate_ref, cw_ref, cb_ref, gb_ref, ng_ref, out_ref,
                  qk_s, h_s, *, n_lat, n_tot):
    lc = ROW_TILE
    t_rows = n_tot * lc
    halo = BF16_SUBLANES
    lane = lax.broadcasted_iota(jnp.int32, (1, LANES), 1)

    cw = cw_ref[...]
    cb = cb_ref[...]
    lane_qk = lax.broadcasted_iota(jnp.int32, (1, 2 * MLSTM_W), 1)
    qscale = jnp.where(lane_qk < MLSTM_W, HEAD_DIM ** -0.5, 1.0)

    def conv_body(c, carry):
        r0 = pl.multiple_of(c * lc, lc)
        main = mqk_ref[0, pl.ds(r0, lc), :].astype(F32)
        ts = pl.multiple_of(jnp.maximum(r0 - halo, 0), halo)
        bs = pl.multiple_of(jnp.minimum(r0 + lc, t_rows - halo), halo)
        top_ok = jnp.logical_and(c != 0, c != n_lat).astype(F32)
        bot_ok = jnp.logical_and(c != n_lat - 1, c != n_tot - 1).astype(F32)
        top = mqk_ref[0, pl.ds(ts, halo), :].astype(F32) * top_ok
        bot = mqk_ref[0, pl.ds(bs, halo), :].astype(F32) * bot_ok
        win = jnp.concatenate([top, main, bot], axis=0)
        n = lc + 2 * halo
        y = win * cw[CONV_W // 2:CONV_W // 2 + 1]
        for j in range(CONV_W):
            if j != CONV_W // 2:
                y = y + pltpu.roll(win, (CONV_W // 2 - j) % n, 0) * cw[j:j + 1]
        y = y[halo:halo + lc] + cb
        qk_s[pl.ds(r0, lc), :] = (y * jax.nn.sigmoid(y) * qscale).astype(BF16)
        return carry

    lax.fori_loop(0, n_tot, conv_body, 0)

    row_i = lax.broadcasted_iota(jnp.int32, (lc, lc), 0)
    col_i = lax.broadcasted_iota(jnp.int32, (lc, lc), 1)
    masks = (col_i <= row_i, col_i >= row_i)
    tris = (masks[0].astype(BF16), masks[1].astype(BF16))
    even = lane < HEAD_DIM
    sels = (even, jnp.logical_not(even))
    one_cols = ((lane == HEAD_DIM).astype(BF16), (lane == 0).astype(BF16))
    den_lane = (HEAD_DIM, 0)
    gb = gb_ref[...]
    h_s[...] = jnp.zeros_like(h_s)
    n_streams = 2 * MLSTM_HEADS

    def step(j, carry):
        cns, ms = carry
        new_cns, new_ms = list(cns), list(ms)
        for d in range(2):
            c = (j + n_lat) % n_tot if d == 0 else n_tot - 1 - j
            r0 = pl.multiple_of(c * lc, lc)
            gates = gate_ref[0, pl.ds(r0, lc), :] + gb
            cum = _dot_split_rhs(tris[d], jax.nn.log_sigmoid(gates))
            gates_t = gates.T
            cum_t = cum.T
            qk = qk_s[pl.ds(r0, lc), :]
            vv = mv_ref[0, pl.ds(r0, lc), :]
            last = lc - 1 if d == 0 else 0
            for p in range(MLSTM_HEADS // 2):
                q128 = qk[:, p * LANES:(p + 1) * LANES]
                k128 = qk[:, MLSTM_W + p * LANES:MLSTM_W + (p + 1) * LANES]
                v128 = vv[:, p * LANES:(p + 1) * LANES]
                halves = []
                for e in range(2):
                    hh = 2 * p + e
                    sidx = d * MLSTM_HEADS + hh
                    ci = d * MLSTM_HEADS + hh
                    cf = 2 * MLSTM_HEADS + d * MLSTM_HEADS + hh
                    qm = jnp.where(sels[e], q128, jnp.zeros_like(q128))
                    v1 = jnp.where(sels[e], v128, one_cols[e])
                    i_col = gates[:, ci:ci + 1]
                    a_col = cum[:, cf:cf + 1]
                    i_row = gates_t[ci:ci + 1, :]
                    a_row = cum_t[cf:cf + 1, :]
                    m = ms[sidx]
                    cn = cns[sidx]
                    dm = jnp.where(masks[d], a_col + (i_row - a_row), -jnp.inf)
                    inter = a_col + m
                    m_t = jnp.maximum(jnp.max(dm, axis=1, keepdims=True), inter)
                    w = jnp.exp(dm - m_t)
                    w_prev = jnp.exp(inter - m_t)
                    sw = (_dot_nt(qm, k128) * w).astype(BF16)
                    nd = _dot(sw, v1) + w_prev * _dot(qm, cn.astype(BF16))
                    den = nd[:, den_lane[e]:den_lane[e] + 1]
                    halves.append(nd * (1.0 / jnp.maximum(jnp.abs(den), jnp.exp(-m_t))))
                    b_last = cum[last:last + 1, cf:cf + 1]
                    logu = b_last - a_col + i_col
                    m_new = jnp.maximum(b_last + m, jnp.max(logu, axis=0, keepdims=True))
                    uk = (k128.astype(F32) * jnp.exp(logu - m_new)).astype(BF16)
                    new_cns[sidx] = jnp.exp(b_last + m - m_new) * cn + _dot_tn(uk, v1)
                    new_ms[sidx] = m_new
                h_s[pl.ds(r0, lc), p * LANES:(p + 1) * LANES] += jnp.where(even, halves[0], halves[1])
        return tuple(new_cns), tuple(new_ms)

    init = (tuple(jnp.zeros((LANES, LANES), F32) for _ in range(n_streams)),
            tuple(jnp.zeros((1, 1), F32) for _ in range(n_streams)))
    lax.fori_loop(0, n_tot, step, init)

    head_mean = _head_mean_matrix(MLSTM_W)
    ng = ng_ref[...]

    def out_body(c, carry):
        r0 = pl.multiple_of(c * lc, lc)
        hb = h_s[pl.ds(r0, lc), :]
        hc = hb - _dot_split_lhs(hb, head_mean)
        var = _dot_split_lhs(hc * hc, head_mean)
        o = mo_ref[0, pl.ds(r0, lc), :].astype(F32)
        out_ref[0, pl.ds(r0, lc), :] = (hc * lax.rsqrt(var + LN_EPS) * ng * jax.nn.sigmoid(o)).astype(BF16)
        return carry

    lax.fori_loop(0, n_tot, out_body, 0)


def _mlstm_call(mqk, mv, mo, gate, cw, cb, gb, ng, n_lat):
    b, t, _ = mqk.shape
    n_tot = t // ROW_TILE
    seq = lambda width: pl.BlockSpec((1, t, width), lambda bi: (bi, 0, 0))
    full = lambda a: pl.BlockSpec(a.shape, lambda bi: (0, 0))
    return pl.pallas_call(
        functools.partial(_mlstm_kernel, n_lat=n_lat, n_tot=n_tot),
        grid=(b,),
        in_specs=[seq(2 * MLSTM_W), seq(MLSTM_W), seq(MLSTM_W), seq(LANES),
                  full(cw), full(cb), full(gb), full(ng)],
        out_specs=seq(MLSTM_W),
        out_shape=jax.ShapeDtypeStruct((b, t, MLSTM_W), BF16),
        scratch_shapes=[pltpu.VMEM((t, 2 * MLSTM_W), BF16), pltpu.VMEM((t, MLSTM_W), F32)],
        compiler_params=_cparams("parallel"),
        name="mlstm",
    )(mqk, mv, mo, gate, cw, cb, gb, ng)


def _flash_update(qm, k, v1, m, acc):
    s = _dot_nt(qm, k)
    m_new = jnp.maximum(m, jnp.max(s, axis=1, keepdims=True))
    p = jnp.exp(s - m_new).astype(BF16)
    return m_new, jnp.exp(m - m_new) * acc + _dot(p, v1)


def _diff_attn_kernel(q_ref, k_ref, v_ref, lam_ref, g_ref, o_ref, *, n_lat, n_tot, lam_init):
    tk = ROW_TILE
    tq = q_ref.shape[1]
    dv = 2 * HEAD_DIM
    qi = pl.program_id(2)
    lane = lax.broadcasted_iota(jnp.int32, (1, LANES), 1)
    q = q_ref[0]
    zero = jnp.zeros_like(q)
    qms = (jnp.where(lane < HEAD_DIM, q, zero), jnp.where(lane >= HEAD_DIM, q, zero))
    ones_blk = jnp.broadcast_to((lane == 0).astype(BF16), (tk, LANES))

    def body(c, carry):
        r0 = pl.multiple_of(c * tk, tk)
        k = k_ref[0, pl.ds(r0, tk), :]
        v1 = jnp.concatenate([v_ref[0, pl.ds(r0, tk), :], ones_blk], axis=1)
        m1, a1, m2, a2 = carry
        m1, a1 = _flash_update(qms[0], k, v1, m1, a1)
        m2, a2 = _flash_update(qms[1], k, v1, m2, a2)
        return m1, a1, m2, a2

    minf = jnp.full((tq, 1), -jnp.inf, F32)
    zacc = jnp.zeros((tq, dv + LANES), F32)
    first = jnp.where(qi < n_lat, 0, n_lat)
    _, a1, _, a2 = lax.fori_loop(first, n_tot, body, (minf, zacc, minf, zacc))

    lv = lam_ref[0]
    lam = (jnp.exp(jnp.sum(lv[0:1] * lv[1:2], axis=1, keepdims=True))
           - jnp.exp(jnp.sum(lv[2:3] * lv[3:4], axis=1, keepdims=True)) + lam_init)
    o = a1[:, :dv] * (1.0 / a1[:, dv:dv + 1]) - lam * (a2[:, :dv] * (1.0 / a2[:, dv:dv + 1]))
    ms = jnp.mean(o * o, axis=1, keepdims=True)
    o_ref[0] = (o * lax.rsqrt(ms + LN_EPS) * (g_ref[...] * (1.0 - lam_init))).astype(BF16)


def _diff_attn_call(dq, dk, dv, lam_vecs, g, n_lat, lam_init):
    b, t, _ = dq.shape
    n_tot = t // ROW_TILE
    tq = ROW_TILE
    return pl.pallas_call(
        functools.partial(_diff_attn_kernel, n_lat=n_lat, n_tot=n_tot, lam_init=lam_init),
        grid=(b, DIFF_HEADS, n_tot),
        in_specs=[pl.BlockSpec((1, tq, LANES), lambda bi, h, i: (bi, i, h)),
                  pl.BlockSpec((1, t, LANES), lambda bi, h, i: (bi, 0, h)),
                  pl.BlockSpec((1, t, LANES), lambda bi, h, i: (bi, 0, h)),
                  pl.BlockSpec((1, 4, HEAD_DIM), lambda bi, h, i: (0, 0, 0)),
                  pl.BlockSpec((1, LANES), lambda bi, h, i: (0, 0))],
        out_specs=pl.BlockSpec((1, tq, LANES), lambda bi, h, i: (bi, i, h)),
        out_shape=jax.ShapeDtypeStruct((b, t, DIFF_W), BF16),
        compiler_params=_cparams("parallel", "parallel", "arbitrary"),
        name="diff_attn",
    )(dq, dk, dv, lam_vecs, g)


def _gqa_attn_kernel(q_ref, k_ref, v_ref, o_ref, *, n_lat, n_tot):
    tk = ROW_TILE
    tq = q_ref.shape[1]
    qi = pl.program_id(1)
    lane = lax.broadcasted_iota(jnp.int32, (1, LANES), 1)
    sels = (lane < HEAD_DIM, lane >= HEAD_DIM)
    q = q_ref[0]
    n_pairs = GQA_W // LANES
    qms = []
    for p in range(n_pairs):
        qp = q[:, p * LANES:(p + 1) * LANES]
        for e in range(GQA_KV_HEADS):
            qms.append(jnp.where(sels[e], qp, jnp.zeros_like(qp)))

    def body(c, carry):
        r0 = pl.multiple_of(c * tk, tk)
        k = k_ref[0, pl.ds(r0, tk), :]
        v = v_ref[0, pl.ds(r0, tk), :]
        out = []
        for idx in range(n_pairs * GQA_KV_HEADS):
            e = idx % GQA_KV_HEADS
            m, acc = carry[idx]
            out.append(_flash_update(qms[idx], k, v[:, e * LANES:(e + 1) * LANES], m, acc))
        return tuple(out)

    minf = jnp.full((tq, 1), -jnp.inf, F32)
    zacc = jnp.zeros((tq, LANES), F32)
    first = jnp.where(qi < n_lat, 0, n_lat)
    res = lax.fori_loop(first, n_tot, body, tuple((minf, zacc) for _ in range(n_pairs * GQA_KV_HEADS)))
    den_lane = (HEAD_DIM, 0)
    for p in range(n_pairs):
        halves = []
        for e in range(GQA_KV_HEADS):
            acc = res[p * GQA_KV_HEADS + e][1]
            halves.append(acc * (1.0 / acc[:, den_lane[e]:den_lane[e] + 1]))
        o_ref[0, :, p * LANES:(p + 1) * LANES] = jnp.where(sels[0], halves[0], halves[1]).astype(BF16)


def _gqa_attn_call(gq, gk, gv, n_lat):
    b, t, _ = gq.shape
    n_tot = t // ROW_TILE
    tq = ROW_TILE
    return pl.pallas_call(
        functools.partial(_gqa_attn_kernel, n_lat=n_lat, n_tot=n_tot),
        grid=(b, n_tot),
        in_specs=[pl.BlockSpec((1, tq, GQA_W), lambda bi, i: (bi, i, 0)),
                  pl.BlockSpec((1, t, GQA_KV_W), lambda bi, i: (bi, 0, 0)),
                  pl.BlockSpec((1, t, 2 * LANES), lambda bi, i: (bi, 0, 0))],
        out_specs=pl.BlockSpec((1, tq, GQA_W), lambda bi, i: (bi, i, 0)),
        out_shape=jax.ShapeDtypeStruct((b, t, GQA_W), BF16),
        compiler_params=_cparams("parallel", "arbitrary"),
        name="gqa_attn",
    )(gq, gk, gv)


def _outproj_kernel(a_ref, bd_ref, cg_ref, x_ref, mod_ref, w_ref, g_ref, b_ref, o_ref, *, alpha):
    y = (_dot(a_ref[0], w_ref[0:MLSTM_W, :])
         + _dot(bd_ref[0], w_ref[MLSTM_W:MLSTM_W + DIFF_W, :])
         + _dot(cg_ref[0], w_ref[MLSTM_W + DIFF_W:, :]))
    mod = mod_ref[0, 0]
    o_ref[0] = _layer_norm(alpha * x_ref[0] + mod[2:3] * y, g_ref[...], b_ref[...])


def _outproj_call(a, bd, cg, xs, mod, w, g, bias, n_lat, n_rows, alpha):
    b, _, d = xs.shape
    tm = ROW_TILE
    row = lambda width: pl.BlockSpec((1, tm, width), lambda bi, i: (bi, i, 0))
    vec = pl.BlockSpec((1, d), lambda bi, i: (0, 0))
    return pl.pallas_call(
        functools.partial(_outproj_kernel, alpha=alpha),
        grid=(b, n_rows // tm),
        in_specs=[row(MLSTM_W), row(DIFF_W), row(GQA_W), row(d),
                  pl.BlockSpec((1, 1, 6, d), lambda bi, i: (bi, jnp.minimum(i // n_lat, 1), 0, 0)),
                  pl.BlockSpec(w.shape, lambda bi, i: (0, 0)), vec, vec],
        out_specs=row(d),
        out_shape=jax.ShapeDtypeStruct((b, n_rows, d), F32),
        compiler_params=_cparams("parallel", "parallel"),
        name="out_proj_ln",
    )(a, bd, cg, xs, mod, w, g, bias)


def _ffn_kernel(x_ref, mod_ref, wi_ref, wo_ref, g_ref, b_ref, o_ref, *, alpha, d_ff, chunk):
    x = x_ref[0]
    mod = mod_ref[0, 0]
    xm = (x * (1.0 + mod[4:5]) + mod[3:4]).astype(BF16)
    acc = jnp.zeros(x.shape, F32)
    for c in range(d_ff // chunk):
        gate = _dot(xm, wi_ref[:, c * chunk:(c + 1) * chunk])
        up = _dot(xm, wi_ref[:, d_ff + c * chunk:d_ff + (c + 1) * chunk])
        act = (gate * jax.nn.sigmoid(gate) * up).astype(BF16)
        acc = acc + _dot(act, wo_ref[c * chunk:(c + 1) * chunk, :])
    o_ref[0] = _layer_norm(alpha * x + mod[5:6] * acc, g_ref[...], b_ref[...])


def _ffn_call(xs, mod, wi, wo, g, bias, n_lat, alpha):
    b, n_rows, d = xs.shape
    d_ff = wo.shape[0]
    tm = ROW_TILE
    row = pl.BlockSpec((1, tm, d), lambda bi, i: (bi, i, 0))
    vec = pl.BlockSpec((1, d), lambda bi, i: (0, 0))
    return pl.pallas_call(
        functools.partial(_ffn_kernel, alpha=alpha, d_ff=d_ff, chunk=2 * LANES),
        grid=(b, n_rows // tm),
        in_specs=[row,
                  pl.BlockSpec((1, 1, 6, d), lambda bi, i: (bi, jnp.minimum(i // n_lat, 1), 0, 0)),
                  pl.BlockSpec(wi.shape, lambda bi, i: (0, 0)),
                  pl.BlockSpec(wo.shape, lambda bi, i: (0, 0)), vec, vec],
        out_specs=row,
        out_shape=jax.ShapeDtypeStruct((b, n_rows, d), F32),
        compiler_params=_cparams("parallel", "parallel"),
        name="ffn_ln",
    )(xs, mod, wi, wo, g, bias)


def _prep_w_in(w):
    zeros = lambda n: jnp.zeros((w.shape[0], n), w.dtype)
    o_gate = 4 * MLSTM_W
    o_dq = o_gate + 4 * MLSTM_HEADS
    o_gq = o_dq + 3 * DIFF_W
    o_gk = o_gq + GQA_W
    o_gv = o_gk + GQA_KV_W
    gq = w[:, o_gq:o_gk].reshape(-1, 2, 2, HEAD_DIM).transpose(0, 2, 1, 3).reshape(-1, GQA_W)
    gv = w[:, o_gv:o_gv + GQA_KV_W]
    cols = [w[:, :o_gate], w[:, o_dq:o_gq], gq, w[:, o_gk:o_gv],
            gv[:, :HEAD_DIM], zeros(LANES), gv[:, HEAD_DIM:],
            w[:, o_gate:o_dq], zeros(LANES - 4 * MLSTM_HEADS)]
    out = jnp.concatenate(cols, axis=1)
    assert out.shape[1] == IN_COLS_PADDED
    return out.astype(BF16)


def _prep_w_out(w):
    o_c = MLSTM_W + DIFF_W
    gq = w[o_c:].reshape(2, 2, HEAD_DIM, -1).transpose(1, 0, 2, 3).reshape(GQA_W, -1)
    return jnp.concatenate([w[:o_c], gq], axis=0).astype(BF16)


def _rope_tables(s, n_ctx_rows):
    pos = jnp.arange(s, dtype=jnp.int32)
    row = (pos // GRID_W).astype(F32)
    col = (pos % GRID_W).astype(F32)
    n_freq = HEAD_DIM // 4
    inv = ROPE_THETA ** (-jnp.arange(n_freq, dtype=F32) / n_freq)
    ar = row[:, None] * inv
    ac = col[:, None] * inv
    ang = jnp.concatenate([ar, ar, ac, ac], axis=-1)
    ang = jnp.concatenate([ang, ang], axis=-1)
    first = (jnp.arange(LANES) % (2 * ROPE_HALF)) < ROPE_HALF
    cos, sin = jnp.cos(ang), jnp.sin(ang)
    sa = jnp.where(first, -sin, 0.0)
    sb = jnp.where(first, 0.0, sin)
    pad = lambda a, v: jnp.concatenate([a, jnp.full((n_ctx_rows, LANES), v, F32)], axis=0)
    return pad(cos, 1.0), pad(sa, 0.0), pad(sb, 0.0)


def kernel(x, c, ctx, c_ctx, w_ada, b_ada, w_in, mlstm_conv_w, mlstm_conv_b, mlstm_gate_b, mlstm_norm_g, diff_lambda, diff_norm_g, gqa_q_norm_g, gqa_k_norm_g, w_out, ln1_g, ln1_b, w_ffn_in, w_ffn_out, ln2_g, ln2_b):
    b, s, d = x.shape
    n_ctx_rows = ctx.shape[1]
    depth = w_in.shape[0]
    assert s % ROW_TILE == 0 and n_ctx_rows % ROW_TILE == 0 and s % GRID_W == 0
    n_lat = s // ROW_TILE
    t = s + n_ctx_rows
    alpha = (2 * depth) ** 0.25

    ada_rows = -(-(b + 1) // 8) * 8
    cvec = jnp.concatenate([c, c_ctx[None, :], jnp.zeros((ada_rows - b - 1, d), F32)], axis=0)
    mod_all = _ada_call(cvec, w_ada, b_ada)
    mod_lat = mod_all[:, :b].reshape(depth, b, 1, 6, d)
    mod_ctx = jnp.broadcast_to(mod_all[:, b].reshape(depth, 1, 1, 6, d), (depth, b, 1, 6, d))
    mod_all = jnp.concatenate([mod_lat, mod_ctx], axis=2)

    tabs = _rope_tables(s, n_ctx_rows)
    tile2 = lambda v, n: jnp.tile(v, n).reshape(1, -1)
    xs = jnp.concatenate([x, ctx], axis=1)

    for l in range(depth):
        last = l == depth - 1
        lam_init = 0.8 - 0.6 * math.exp(-0.3 * l)
        mod = mod_all[l]
        (mqk, mv, mo, gate, dq, dk, dv, gq, gk, gv) = _inproj_call(
            xs, mod, _prep_w_in(w_in[l]), tabs,
            tile2(gqa_q_norm_g[l], LANES // HEAD_DIM), tile2(gqa_k_norm_g[l], LANES // HEAD_DIM), n_lat)
        cw = jnp.concatenate([mlstm_conv_w[l], jnp.zeros((8 - CONV_W, 2 * MLSTM_W), F32)], axis=0)
        gb = jnp.concatenate([mlstm_gate_b[l], jnp.zeros((LANES - 4 * MLSTM_HEADS,), F32)]).reshape(1, LANES)
        a = _mlstm_call(mqk, mv, mo, gate, cw, mlstm_conv_b[l].reshape(1, -1), gb,
                        mlstm_norm_g[l].reshape(1, -1), n_lat)
        bd = _diff_attn_call(dq, dk, dv, diff_lambda[l][None], diff_norm_g[l].reshape(1, -1), n_lat, lam_init)
        cg = _gqa_attn_call(gq, gk, gv, n_lat)
        n_rows = s if last else t
        xs = _outproj_call(a, bd, cg, xs, mod, _prep_w_out(w_out[l]),
                           ln1_g[l].reshape(1, -1), ln1_b[l].reshape(1, -1), n_lat, n_rows, alpha)
        xs = _ffn_call(xs, mod, w_ffn_in[l].astype(BF16), w_ffn_out[l].astype(BF16),
                       ln2_g[l].reshape(1, -1), ln2_b[l].reshape(1, -1), n_lat, alpha)
    return xs
```

```python
import functools
import math

import jax
import jax.numpy as jnp
from jax import lax
from jax.experimental import pallas as pl
from jax.experimental.pallas import tpu as pltpu

F32 = jnp.float32
BF16 = jnp.bfloat16

HEAD_DIM = 64
MLSTM_HEADS = 4
DIFF_HEADS = 4
GQA_HEADS = 4
GQA_KV_HEADS = 2
GRID_W = 64
CONV_W = 5
ROPE_THETA = 10000.0
LN_EPS = 1e-5

MLSTM_W = MLSTM_HEADS * HEAD_DIM
DIFF_W = DIFF_HEADS * 2 * HEAD_DIM
GQA_W = GQA_HEADS * HEAD_DIM
GQA_KV_W = GQA_KV_HEADS * HEAD_DIM

LANES = 128
BF16_SUBLANES = 16
VMEM_LIMIT_BYTES = 56 * 1024 * 1024

ROW_TILE = 256
ROPE_HALF = HEAD_DIM // 4
ATTN_TQ = 256
ATTN_TK = 1024
ATTN_Q_SCALE = HEAD_DIM ** -0.5 * math.log2(math.e)

C_MQK, C_MV, C_MO = 0, 512, 768
C_DQ, C_DK, C_DV = 1024, 1536, 2048
C_GQ, C_GK, C_GV, C_GATE = 2560, 2816, 2944, 3200
IN_COLS_PADDED = 3328


def _cparams(*sem):
    return pltpu.CompilerParams(dimension_semantics=sem, vmem_limit_bytes=VMEM_LIMIT_BYTES)


def _dot(a, b):
    return jnp.dot(a, b, preferred_element_type=F32)


def _dot_nt(a, b):
    return lax.dot_general(a, b, (((1,), (1,)), ((), ())), preferred_element_type=F32)


def _dot_tn(a, b):
    return lax.dot_general(a, b, (((0,), (0,)), ((), ())), preferred_element_type=F32)


def _split_bf16(a):
    hi = a.astype(BF16)
    lo = (a - hi.astype(F32)).astype(BF16)
    return hi, lo


def _dot_split_lhs(a, m):
    hi, lo = _split_bf16(a)
    return _dot(hi, m) + _dot(lo, m)


def _dot_split_rhs(m, a):
    hi, lo = _split_bf16(a)
    return _dot(m, hi) + _dot(m, lo)


def _head_mean_matrix(width):
    r = lax.broadcasted_iota(jnp.int32, (width, width), 0) // HEAD_DIM
    c = lax.broadcasted_iota(jnp.int32, (width, width), 1) // HEAD_DIM
    return jnp.where(r == c, 1.0 / HEAD_DIM, 0.0).astype(BF16)


def _layer_norm(z, g, b):
    mu = jnp.mean(z, axis=-1, keepdims=True)
    zc = z - mu
    var = jnp.mean(zc * zc, axis=-1, keepdims=True)
    return zc * lax.rsqrt(var + LN_EPS) * g + b


def _ada_kernel(c_ref, w_ref, b_ref, o_ref):
    c = c_ref[...]
    o_ref[0] = _dot(c * jax.nn.sigmoid(c), w_ref[0]) + b_ref[0]


def _ada_call(cvec, w_ada, b_ada):
    depth, d, six_d = w_ada.shape
    rows = cvec.shape[0]
    bn = 1024
    return pl.pallas_call(
        _ada_kernel,
        grid=(depth, six_d // bn),
        in_specs=[pl.BlockSpec((rows, d), lambda l, j: (0, 0)),
                  pl.BlockSpec((1, d, bn), lambda l, j: (l, 0, j)),
                  pl.BlockSpec((1, 1, bn), lambda l, j: (l, 0, j))],
        out_specs=pl.BlockSpec((1, rows, bn), lambda l, j: (l, 0, j)),
        out_shape=jax.ShapeDtypeStruct((depth, rows, six_d), F32),
        compiler_params=_cparams("parallel", "parallel"),
        name="adaln_mod",
    )(cvec, w_ada, b_ada.reshape(depth, 1, six_d))


def _inproj_kernel(x_ref, mod_ref, w_ref, cos_ref, sa_ref, sb_ref, qg_ref, kg_ref,
                   mqk_ref, mv_ref, mo_ref, gate_ref, dq_ref, dk_ref, dv_ref,
                   gq_ref, gk_ref, gv_ref):
    x = x_ref[0]
    mod = mod_ref[0, 0]
    xm = (x * (1.0 + mod[1:2]) + mod[0:1]).astype(BF16)
    cos, sa, sb = cos_ref[...], sa_ref[...], sb_ref[...]
    scale = ATTN_Q_SCALE

    def proj(lo, n):
        return _dot(xm, w_ref[:, lo:lo + n])

    def rope(a):
        return (a * cos + pltpu.roll(a, LANES - ROPE_HALF, 1) * sa
                + pltpu.roll(a, ROPE_HALF, 1) * sb)

    mqk_ref[0] = proj(C_MQK, 2 * MLSTM_W).astype(BF16)
    mv_ref[0] = proj(C_MV, MLSTM_W).astype(BF16)
    mo_ref[0] = proj(C_MO, MLSTM_W).astype(BF16)
    gate_ref[0] = proj(C_GATE, LANES)

    acc = proj(C_DQ, DIFF_W)
    for j in range(DIFF_W // LANES):
        dq_ref[0, :, j * LANES:(j + 1) * LANES] = (
            rope(acc[:, j * LANES:(j + 1) * LANES]) * scale).astype(BF16)
    acc = proj(C_DK, DIFF_W)
    for j in range(DIFF_W // LANES):
        dk_ref[0, :, j * LANES:(j + 1) * LANES] = rope(acc[:, j * LANES:(j + 1) * LANES]).astype(BF16)
    dv_ref[0] = proj(C_DV, DIFF_W).astype(BF16)

    head_mean = _head_mean_matrix(LANES)

    def rms(a, g):
        ms = _dot_split_lhs(a * a, head_mean)
        return a * lax.rsqrt(ms + LN_EPS) * g

    acc = proj(C_GQ, GQA_W)
    qg = qg_ref[...]
    for j in range(GQA_W // LANES):
        gq_ref[0, :, j * LANES:(j + 1) * LANES] = (
            rope(rms(acc[:, j * LANES:(j + 1) * LANES], qg)) * scale).astype(BF16)
    gk_ref[0] = rope(rms(proj(C_GK, GQA_KV_W), kg_ref[...])).astype(BF16)

    lane = lax.broadcasted_iota(jnp.int32, (1, 2 * LANES), 1)
    ones_col = jnp.logical_or(lane == HEAD_DIM, lane == LANES).astype(F32)
    gv_ref[0] = (proj(C_GV, 2 * LANES) + ones_col).astype(BF16)


def _inproj_call(xs, mod, w, tabs, qg, kg, n_lat):
    b, t, d = xs.shape
    n_tot = t // ROW_TILE
    tm = ROW_TILE
    row = lambda width: pl.BlockSpec((1, tm, width), lambda bi, i: (bi, i, 0))
    tab = pl.BlockSpec((tm, LANES), lambda bi, i: (i, 0))
    vec = pl.BlockSpec((1, LANES), lambda bi, i: (0, 0))
    widths = (2 * MLSTM_W, MLSTM_W, MLSTM_W, LANES, DIFF_W, DIFF_W, DIFF_W, GQA_W, GQA_KV_W, 2 * LANES)
    dtypes = (BF16, BF16, BF16, F32, BF16, BF16, BF16, BF16, BF16, BF16)
    return pl.pallas_call(
        _inproj_kernel,
        grid=(b, n_tot),
        in_specs=[row(d),
                  pl.BlockSpec((1, 1, 6, d), lambda bi, i: (bi, jnp.minimum(i // n_lat, 1), 0, 0)),
                  pl.BlockSpec((d, IN_COLS_PADDED), lambda bi, i: (0, 0)),
                  tab, tab, tab, vec, vec],
        out_specs=[row(wd) for wd in widths],
        out_shape=[jax.ShapeDtypeStruct((b, t, wd), dt) for wd, dt in zip(widths, dtypes)],
        compiler_params=_cparams("parallel", "parallel"),
        name="in_proj",
    )(xs, mod, w, *tabs, qg, kg)


def _mlstm_kernel(mqk_ref, mv_ref, mo_ref, gate_ref, cw_ref, cb_ref, gb_ref, ng_ref, out_ref,
                  qk_s, h_s, *, n_lat, n_tot):
    lc = ROW_TILE
    t_rows = n_tot * lc
    halo = BF16_SUBLANES
    lane = lax.broadcasted_iota(jnp.int32, (1, LANES), 1)

    cw = cw_ref[...]
    cb = cb_ref[...]
    lane_qk = lax.broadcasted_iota(jnp.int32, (1, 2 * MLSTM_W), 1)
    qscale = jnp.where(lane_qk < MLSTM_W, HEAD_DIM ** -0.5, 1.0)

    def conv_body(c, carry):
        r0 = pl.multiple_of(c * lc, lc)
        main = mqk_ref[0, pl.ds(r0, lc), :].astype(F32)
        ts = pl.multiple_of(jnp.maximum(r0 - halo, 0), halo)
        bs = pl.multiple_of(jnp.minimum(r0 + lc, t_rows - halo), halo)
        top_ok = jnp.logical_and(c != 0, c != n_lat).astype(F32)
        bot_ok = jnp.logical_and(c != n_lat - 1, c != n_tot - 1).astype(F32)
        top = mqk_ref[0, pl.ds(ts, halo), :].astype(F32) * top_ok
        bot = mqk_ref[0, pl.ds(bs, halo), :].astype(F32) * bot_ok
        win = jnp.concatenate([top, main, bot], axis=0)
        n = lc + 2 * halo
        y = win * cw[CONV_W // 2:CONV_W // 2 + 1]
        for j in range(CONV_W):
            if j != CONV_W // 2:
                y = y + pltpu.roll(win, (CONV_W // 2 - j) % n, 0) * cw[j:j + 1]
        y = y[halo:halo + lc] + cb
        qk_s[pl.ds(r0, lc), :] = (y * jax.nn.sigmoid(y) * qscale).astype(BF16)
        return carry

    lax.fori_loop(0, n_tot, conv_body, 0)

    row_i = lax.broadcasted_iota(jnp.int32, (lc, lc), 0)
    col_i = lax.broadcasted_iota(jnp.int32, (lc, lc), 1)
    masks = (col_i <= row_i, col_i >= row_i)
    tris = (masks[0].astype(BF16), masks[1].astype(BF16))
    even = lane < HEAD_DIM
    sels = (even, jnp.logical_not(even))
    one_cols = ((lane == HEAD_DIM).astype(BF16), (lane == 0).astype(BF16))
    den_lane = (HEAD_DIM, 0)
    gb = gb_ref[...]
    h_s[...] = jnp.zeros_like(h_s)
    n_streams = 2 * MLSTM_HEADS

    def step(j, carry):
        cns, ms = carry
        new_cns, new_ms = list(cns), list(ms)
        for d in range(2):
            c = (j + n_lat) % n_tot if d == 0 else n_tot - 1 - j
            r0 = pl.multiple_of(c * lc, lc)
            gates = gate_ref[0, pl.ds(r0, lc), :] + gb
            cum = _dot_split_rhs(tris[d], jax.nn.log_sigmoid(gates))
            gates_t = gates.T
            cum_t = cum.T
            qk = qk_s[pl.ds(r0, lc), :]
            vv = mv_ref[0, pl.ds(r0, lc), :]
            last = lc - 1 if d == 0 else 0
            for p in range(MLSTM_HEADS // 2):
                q128 = qk[:, p * LANES:(p + 1) * LANES]
                k128 = qk[:, MLSTM_W + p * LANES:MLSTM_W + (p + 1) * LANES]
                v128 = vv[:, p * LANES:(p + 1) * LANES]
                halves = []
                for e in range(2):
                    hh = 2 * p + e
                    sidx = d * MLSTM_HEADS + hh
                    ci = d * MLSTM_HEADS + hh
                    cf = 2 * MLSTM_HEADS + d * MLSTM_HEADS + hh
                    qm = jnp.where(sels[e], q128, jnp.zeros_like(q128))
                    v1 = jnp.where(sels[e], v128, one_cols[e])
                    i_col = gates[:, ci:ci + 1]
                    a_col = cum[:, cf:cf + 1]
                    i_row = gates_t[ci:ci + 1, :]
                    a_row = cum_t[cf:cf + 1, :]
                    m = ms[sidx]
                    cn = cns[sidx]
                    dm = jnp.where(masks[d], a_col + (i_row - a_row), -jnp.inf)
                    inter = a_col + m
                    m_t = jnp.maximum(jnp.max(dm, axis=1, keepdims=True), inter)
                    w = jnp.exp(dm - m_t)
                    w_prev = jnp.exp(inter - m_t)
                    sw = (_dot_nt(qm, k128) * w).astype(BF16)
                    nd = _dot(sw, v1) + w_prev * _dot(qm, cn.astype(BF16))
                    den = nd[:, den_lane[e]:den_lane[e] + 1]
                    halves.append(nd * (1.0 / jnp.maximum(jnp.abs(den), jnp.exp(-m_t))))
                    b_last = cum[last:last + 1, cf:cf + 1]
                    logu = b_last - a_col + i_col
                    m_new = jnp.maximum(b_last + m, jnp.max(logu, axis=0, keepdims=True))
                    uk = (k128.astype(F32) * jnp.exp(logu - m_new)).astype(BF16)
                    new_cns[sidx] = jnp.exp(b_last + m - m_new) * cn + _dot_tn(uk, v1)
                    new_ms[sidx] = m_new
                h_s[pl.ds(r0, lc), p * LANES:(p + 1) * LANES] += jnp.where(even, halves[0], halves[1])
        return tuple(new_cns), tuple(new_ms)

    init = (tuple(jnp.zeros((LANES, LANES), F32) for _ in range(n_streams)),
            tuple(jnp.zeros((1, 1), F32) for _ in range(n_streams)))
    lax.fori_loop(0, n_tot, step, init)

    head_mean = _head_mean_matrix(MLSTM_W)
    ng = ng_ref[...]

    def out_body(c, carry):
        r0 = pl.multiple_of(c * lc, lc)
        hb = h_s[pl.ds(r0, lc), :]
        hc = hb - _dot_split_lhs(hb, head_mean)
        var = _dot_split_lhs(hc * hc, head_mean)
        o = mo_ref[0, pl.ds(r0, lc), :].astype(F32)
        out_ref[0, pl.ds(r0, lc), :] = (hc * lax.rsqrt(var + LN_EPS) * ng * jax.nn.sigmoid(o)).astype(BF16)
        return carry

    lax.fori_loop(0, n_tot, out_body, 0)


def _mlstm_call(mqk, mv, mo, gate, cw, cb, gb, ng, n_lat):
    b, t, _ = mqk.shape
    n_tot = t // ROW_TILE
    seq = lambda width: pl.BlockSpec((1, t, width), lambda bi: (bi, 0, 0))
    full = lambda a: pl.BlockSpec(a.shape, lambda bi: (0, 0))
    return pl.pallas_call(
        functools.partial(_mlstm_kernel, n_lat=n_lat, n_tot=n_tot),
        grid=(b,),
        in_specs=[seq(2 * MLSTM_W), seq(MLSTM_W), seq(MLSTM_W), seq(LANES),
                  full(cw), full(cb), full(gb), full(ng)],
        out_specs=seq(MLSTM_W),
        out_shape=jax.ShapeDtypeStruct((b, t, MLSTM_W), BF16),
        scratch_shapes=[pltpu.VMEM((t, 2 * MLSTM_W), BF16), pltpu.VMEM((t, MLSTM_W), F32)],
        compiler_params=_cparams("parallel"),
        name="mlstm",
    )(mqk, mv, mo, gate, cw, cb, gb, ng)


def _key_blocks(first, n_rows, tk):
    blocks, r = [], first
    while r < first + n_rows:
        size = min(tk, first + n_rows - r)
        blocks.append((r, size))
        r += size
    return blocks


def _attn_pair_tile(qms, k_ref, v1_fn, blocks):
    ms, accs = [None, None], [None, None]
    for start, size in blocks:
        k = k_ref[0, pl.ds(start, size), :]
        for e in range(2):
            s = _dot_nt(qms[e], k)
            row_max = jnp.max(s, axis=1, keepdims=True)
            m_new = row_max if ms[e] is None else jnp.maximum(ms[e], row_max)
            pv = _dot(jnp.exp2(s - m_new).astype(BF16), v1_fn(e, start, size))
            accs[e] = pv if accs[e] is None else jnp.exp2(ms[e] - m_new) * accs[e] + pv
            ms[e] = m_new
    return accs


def _for_query_tiles(n_lat_rows, n_ctx_rows, tq, tk, tile_fn):
    lat_blocks = _key_blocks(0, n_lat_rows, tk) + _key_blocks(n_lat_rows, n_ctx_rows, tk)
    ctx_blocks = _key_blocks(n_lat_rows, n_ctx_rows, tk)

    def lat_body(i, carry):
        tile_fn(pl.multiple_of(i * tq, tq), tq, lat_blocks)
        return carry

    lax.fori_loop(0, n_lat_rows // tq, lat_body, 0)
    tq_ctx = min(tq, n_ctx_rows)
    for i in range(n_ctx_rows // tq_ctx):
        tile_fn(n_lat_rows + i * tq_ctx, tq_ctx, ctx_blocks)


def _diff_attn_kernel(q_ref, k_ref, v_ref, lam_ref, g_ref, o_ref, *, n_lat_rows, tq, tk, lam_init):
    dv = 2 * HEAD_DIM
    n_ctx_rows = q_ref.shape[1] - n_lat_rows
    lane = lax.broadcasted_iota(jnp.int32, (1, LANES), 1)
    ones_blk = jnp.broadcast_to((lane == 0).astype(BF16), (tk, LANES))
    lv = lam_ref[0]
    lam = (jnp.exp(jnp.sum(lv[0:1] * lv[1:2], axis=1, keepdims=True))
           - jnp.exp(jnp.sum(lv[2:3] * lv[3:4], axis=1, keepdims=True)) + lam_init)
    gain = g_ref[...] * (1.0 - lam_init)

    def v1_fn(e, start, size):
        return jnp.concatenate([v_ref[0, pl.ds(start, size), :], ones_blk[:size]], axis=1)

    def tile_fn(r0, rows, blocks):
        q = q_ref[0, pl.ds(r0, rows), :]
        zero = jnp.zeros_like(q)
        qms = (jnp.where(lane < HEAD_DIM, q, zero), jnp.where(lane >= HEAD_DIM, q, zero))
        a1, a2 = _attn_pair_tile(qms, k_ref, v1_fn, blocks)
        o = a1[:, :dv] * (1.0 / a1[:, dv:dv + 1]) - lam * (a2[:, :dv] * (1.0 / a2[:, dv:dv + 1]))
        ms = jnp.mean(o * o, axis=1, keepdims=True)
        o_ref[0, pl.ds(r0, rows), :] = (o * lax.rsqrt(ms + LN_EPS) * gain).astype(BF16)

    _for_query_tiles(n_lat_rows, n_ctx_rows, tq, tk, tile_fn)


def _diff_attn_call(dq, dk, dv, lam_vecs, g, n_lat, lam_init):
    b, t, _ = dq.shape
    seq = pl.BlockSpec((1, t, LANES), lambda bi, h: (bi, 0, h))
    return pl.pallas_call(
        functools.partial(_diff_attn_kernel, n_lat_rows=n_lat * ROW_TILE, tq=ATTN_TQ, tk=ATTN_TK,
                          lam_init=lam_init),
        grid=(b, DIFF_HEADS),
        in_specs=[seq, seq, seq,
                  pl.BlockSpec((1, 4, HEAD_DIM), lambda bi, h: (0, 0, 0)),
                  pl.BlockSpec((1, LANES), lambda bi, h: (0, 0))],
        out_specs=seq,
        out_shape=jax.ShapeDtypeStruct((b, t, DIFF_W), BF16),
        compiler_params=_cparams("parallel", "parallel"),
        name="diff_attn",
    )(dq, dk, dv, lam_vecs, g)


def _gqa_attn_kernel(q_ref, k_ref, v_ref, o_ref, *, n_lat_rows, tq, tk):
    n_ctx_rows = q_ref.shape[1] - n_lat_rows
    lane = lax.broadcasted_iota(jnp.int32, (1, LANES), 1)
    first_half = lane < HEAD_DIM
    den_lane = (HEAD_DIM, 0)

    def v1_fn(e, start, size):
        return v_ref[0, pl.ds(start, size), e * LANES:(e + 1) * LANES]

    def tile_fn(r0, rows, blocks):
        q = q_ref[0, pl.ds(r0, rows), :]
        zero = jnp.zeros_like(q)
        qms = (jnp.where(first_half, q, zero), jnp.where(first_half, zero, q))
        accs = _attn_pair_tile(qms, k_ref, v1_fn, blocks)
        halves = [acc * (1.0 / acc[:, den_lane[e]:den_lane[e] + 1]) for e, acc in enumerate(accs)]
        o_ref[0, pl.ds(r0, rows), :] = jnp.where(first_half, halves[0], halves[1]).astype(BF16)

    _for_query_tiles(n_lat_rows, n_ctx_rows, tq, tk, tile_fn)


def _gqa_attn_call(gq, gk, gv, n_lat):
    b, t, _ = gq.shape
    pair = pl.BlockSpec((1, t, LANES), lambda bi, p: (bi, 0, p))
    return pl.pallas_call(
        functools.partial(_gqa_attn_kernel, n_lat_rows=n_lat * ROW_TILE, tq=ATTN_TQ, tk=ATTN_TK),
        grid=(b, GQA_W // LANES),
        in_specs=[pair,
                  pl.BlockSpec((1, t, GQA_KV_W), lambda bi, p: (bi, 0, 0)),
                  pl.BlockSpec((1, t, 2 * LANES), lambda bi, p: (bi, 0, 0))],
        out_specs=pair,
        out_shape=jax.ShapeDtypeStruct((b, t, GQA_W), BF16),
        compiler_params=_cparams("parallel", "arbitrary"),
        name="gqa_attn",
    )(gq, gk, gv)


def _outproj_kernel(a_ref, bd_ref, cg_ref, x_ref, mod_ref, w_ref, g_ref, b_ref, o_ref, *, alpha):
    y = (_dot(a_ref[0], w_ref[0:MLSTM_W, :])
         + _dot(bd_ref[0], w_ref[MLSTM_W:MLSTM_W + DIFF_W, :])
         + _dot(cg_ref[0], w_ref[MLSTM_W + DIFF_W:, :]))
    mod = mod_ref[0, 0]
    o_ref[0] = _layer_norm(alpha * x_ref[0] + mod[2:3] * y, g_ref[...], b_ref[...])


def _outproj_call(a, bd, cg, xs, mod, w, g, bias, n_lat, n_rows, alpha):
    b, _, d = xs.shape
    tm = ROW_TILE
    row = lambda width: pl.BlockSpec((1, tm, width), lambda bi, i: (bi, i, 0))
    vec = pl.BlockSpec((1, d), lambda bi, i: (0, 0))
    return pl.pallas_call(
        functools.partial(_outproj_kernel, alpha=alpha),
        grid=(b, n_rows // tm),
        in_specs=[row(MLSTM_W), row(DIFF_W), row(GQA_W), row(d),
                  pl.BlockSpec((1, 1, 6, d), lambda bi, i: (bi, jnp.minimum(i // n_lat, 1), 0, 0)),
                  pl.BlockSpec(w.shape, lambda bi, i: (0, 0)), vec, vec],
        out_specs=row(d),
        out_shape=jax.ShapeDtypeStruct((b, n_rows, d), F32),
        compiler_params=_cparams("parallel", "parallel"),
        name="out_proj_ln",
    )(a, bd, cg, xs, mod, w, g, bias)


def _ffn_kernel(x_ref, mod_ref, wi_ref, wo_ref, g_ref, b_ref, o_ref, *, alpha, d_ff, chunk):
    x = x_ref[0]
    mod = mod_ref[0, 0]
    xm = (x * (1.0 + mod[4:5]) + mod[3:4]).astype(BF16)
    acc = jnp.zeros(x.shape, F32)
    for c in range(d_ff // chunk):
        gate = _dot(xm, wi_ref[:, c * chunk:(c + 1) * chunk])
        up = _dot(xm, wi_ref[:, d_ff + c * chunk:d_ff + (c + 1) * chunk])
        act = (gate * jax.nn.sigmoid(gate) * up).astype(BF16)
        acc = acc + _dot(act, wo_ref[c * chunk:(c + 1) * chunk, :])
    o_ref[0] = _layer_norm(alpha * x + mod[5:6] * acc, g_ref[...], b_ref[...])


def _ffn_call(xs, mod, wi, wo, g, bias, n_lat, alpha):
    b, n_rows, d = xs.shape
    d_ff = wo.shape[0]
    tm = ROW_TILE
    row = pl.BlockSpec((1, tm, d), lambda bi, i: (bi, i, 0))
    vec = pl.BlockSpec((1, d), lambda bi, i: (0, 0))
    return pl.pallas_call(
        functools.partial(_ffn_kernel, alpha=alpha, d_ff=d_ff, chunk=2 * LANES),
        grid=(b, n_rows // tm),
        in_specs=[row,
                  pl.BlockSpec((1, 1, 6, d), lambda bi, i: (bi, jnp.minimum(i // n_lat, 1), 0, 0)),
                  pl.BlockSpec(wi.shape, lambda bi, i: (0, 0)),
                  pl.BlockSpec(wo.shape, lambda bi, i: (0, 0)), vec, vec],
        out_specs=row,
        out_shape=jax.ShapeDtypeStruct((b, n_rows, d), F32),
        compiler_params=_cparams("parallel", "parallel"),
        name="ffn_ln",
    )(xs, mod, wi, wo, g, bias)


def _prep_w_in(w):
    zeros = lambda n: jnp.zeros((w.shape[0], n), w.dtype)
    o_gate = 4 * MLSTM_W
    o_dq = o_gate + 4 * MLSTM_HEADS
    o_gq = o_dq + 3 * DIFF_W
    o_gk = o_gq + GQA_W
    o_gv = o_gk + GQA_KV_W
    gq = w[:, o_gq:o_gk].reshape(-1, 2, 2, HEAD_DIM).transpose(0, 2, 1, 3).reshape(-1, GQA_W)
    gv = w[:, o_gv:o_gv + GQA_KV_W]
    cols = [w[:, :o_gate], w[:, o_dq:o_gq], gq, w[:, o_gk:o_gv],
            gv[:, :HEAD_DIM], zeros(LANES), gv[:, HEAD_DIM:],
            w[:, o_gate:o_dq], zeros(LANES - 4 * MLSTM_HEADS)]
    out = jnp.concatenate(cols, axis=1)
    assert out.shape[1] == IN_COLS_PADDED
    return out.astype(BF16)


def _prep_w_out(w):
    o_c = MLSTM_W + DIFF_W
    gq = w[o_c:].reshape(2, 2, HEAD_DIM, -1).transpose(1, 0, 2, 3).reshape(GQA_W, -1)
    return jnp.concatenate([w[:o_c], gq], axis=0).astype(BF16)


def _rope_tables(s, n_ctx_rows):
    pos = jnp.arange(s, dtype=jnp.int32)
    row = (pos // GRID_W).astype(F32)
    col = (pos % GRID_W).astype(F32)
    n_freq = HEAD_DIM // 4
    inv = ROPE_THETA ** (-jnp.arange(n_freq, dtype=F32) / n_freq)
    ar = row[:, None] * inv
    ac = col[:, None] * inv
    ang = jnp.concatenate([ar, ar, ac, ac], axis=-1)
    ang = jnp.concatenate([ang, ang], axis=-1)
    first = (jnp.arange(LANES) % (2 * ROPE_HALF)) < ROPE_HALF
    cos, sin = jnp.cos(ang), jnp.sin(ang)
    sa = jnp.where(first, -sin, 0.0)
    sb = jnp.where(first, 0.0, sin)
    pad = lambda a, v: jnp.concatenate([a, jnp.full((n_ctx_rows, LANES), v, F32)], axis=0)
    return pad(cos, 1.0), pad(sa, 0.0), pad(sb, 0.0)


def kernel(x, c, ctx, c_ctx, w_ada, b_ada, w_in, mlstm_conv_w, mlstm_conv_b, mlstm_gate_b, mlstm_norm_g, diff_lambda, diff_norm_g, gqa_q_norm_g, gqa_k_norm_g, w_out, ln1_g, ln1_b, w_ffn_in, w_ffn_out, ln2_g, ln2_b):
    b, s, d = x.shape
    n_ctx_rows = ctx.shape[1]
    depth = w_in.shape[0]
    assert s % ROW_TILE == 0 and n_ctx_rows % ROW_TILE == 0 and s % GRID_W == 0
    n_lat = s // ROW_TILE
    t = s + n_ctx_rows
    alpha = (2 * depth) ** 0.25

    ada_rows = -(-(b + 1) // 8) * 8
    cvec = jnp.concatenate([c, c_ctx[None, :], jnp.zeros((ada_rows - b - 1, d), F32)], axis=0)
    mod_all = _ada_call(cvec, w_ada, b_ada)
    mod_lat = mod_all[:, :b].reshape(depth, b, 1, 6, d)
    mod_ctx = jnp.broadcast_to(mod_all[:, b].reshape(depth, 1, 1, 6, d), (depth, b, 1, 6, d))
    mod_all = jnp.concatenate([mod_lat, mod_ctx], axis=2)

    tabs = _rope_tables(s, n_ctx_rows)
    tile2 = lambda v, n: jnp.tile(v, n).reshape(1, -1)
    xs = jnp.concatenate([x, ctx], axis=1)

    for l in range(depth):
        last = l == depth - 1
        lam_init = 0.8 - 0.6 * math.exp(-0.3 * l)
        mod = mod_all[l]
        (mqk, mv, mo, gate, dq, dk, dv, gq, gk, gv) = _inproj_call(
            xs, mod, _prep_w_in(w_in[l]), tabs,
            tile2(gqa_q_norm_g[l], LANES // HEAD_DIM), tile2(gqa_k_norm_g[l], LANES // HEAD_DIM), n_lat)
        cw = jnp.concatenate([mlstm_conv_w[l], jnp.zeros((8 - CONV_W, 2 * MLSTM_W), F32)], axis=0)
        gb = jnp.concatenate([mlstm_gate_b[l], jnp.zeros((LANES - 4 * MLSTM_HEADS,), F32)]).reshape(1, LANES)
        a = _mlstm_call(mqk, mv, mo, gate, cw, mlstm_conv_b[l].reshape(1, -1), gb,
                        mlstm_norm_g[l].reshape(1, -1), n_lat)
        bd = _diff_attn_call(dq, dk, dv, diff_lambda[l][None], diff_norm_g[l].reshape(1, -1), n_lat, lam_init)
        cg = _gqa_attn_call(gq, gk, gv, n_lat)
        n_rows = s if last else t
        xs = _outproj_call(a, bd, cg, xs, mod, _prep_w_out(w_out[l]),
                           ln1_g[l].reshape(1, -1), ln1_b[l].reshape(1, -1), n_lat, n_rows, alpha)
        xs = _ffn_call(xs, mod, w_ffn_in[l].astype(BF16), w_ffn_out[l].astype(BF16),
                       ln2_g[l].reshape(1, -1), ln2_b[l].reshape(1, -1), n_lat, alpha)
    return xs
```

```python
import functools
import math

import jax
import jax.numpy as jnp
from jax import lax
from jax.experimental import pallas as pl
from jax.experimental.pallas import tpu as pltpu

F32 = jnp.float32
BF16 = jnp.bfloat16

HEAD_DIM = 64
MLSTM_HEADS = 4
DIFF_HEADS = 4
GQA_HEADS = 4
GQA_KV_HEADS = 2
GRID_W = 64
CONV_W = 5
ROPE_THETA = 10000.0
LN_EPS = 1e-5

MLSTM_W = MLSTM_HEADS * HEAD_DIM
DIFF_W = DIFF_HEADS * 2 * HEAD_DIM
GQA_W = GQA_HEADS * HEAD_DIM
GQA_KV_W = GQA_KV_HEADS * HEAD_DIM

LANES = 128
BF16_SUBLANES = 16
VMEM_LIMIT_BYTES = 56 * 1024 * 1024

ROW_TILE = 256
ROPE_HALF = HEAD_DIM // 4
ATTN_TQ = 256
DIFF_TK = 1024
GQA_TK = 4096
ATTN_Q_SCALE = HEAD_DIM ** -0.5 * math.log2(math.e)

C_MQK, C_MV, C_MO = 0, 512, 768
C_DQ, C_DK, C_DV = 1024, 1536, 2048
C_GQ, C_GK, C_GV, C_GATE = 2560, 2816, 2944, 3200
IN_COLS_PADDED = 3328


def _cparams(*sem):
    return pltpu.CompilerParams(dimension_semantics=sem, vmem_limit_bytes=VMEM_LIMIT_BYTES)


def _dot(a, b):
    return jnp.dot(a, b, preferred_element_type=F32)


def _dot_nt(a, b):
    return lax.dot_general(a, b, (((1,), (1,)), ((), ())), preferred_element_type=F32)


def _split_bf16(a):
    hi = a.astype(BF16)
    lo = (a - hi.astype(F32)).astype(BF16)
    return hi, lo


def _dot_split_lhs(a, m):
    hi, lo = _split_bf16(a)
    return _dot(hi, m) + _dot(lo, m)


def _dot_split_rhs(m, a):
    hi, lo = _split_bf16(a)
    return _dot(m, hi) + _dot(m, lo)


def _head_mean_matrix(width):
    r = lax.broadcasted_iota(jnp.int32, (width, width), 0) // HEAD_DIM
    c = lax.broadcasted_iota(jnp.int32, (width, width), 1) // HEAD_DIM
    return jnp.where(r == c, 1.0 / HEAD_DIM, 0.0).astype(BF16)


def _layer_norm(z, g, b):
    mu = jnp.mean(z, axis=-1, keepdims=True)
    zc = z - mu
    var = jnp.mean(zc * zc, axis=-1, keepdims=True)
    return zc * lax.rsqrt(var + LN_EPS) * g + b


def _ada_kernel(c_ref, w_ref, b_ref, o_ref):
    c = c_ref[...]
    o_ref[0] = _dot(c * jax.nn.sigmoid(c), w_ref[0]) + b_ref[0]


def _ada_call(cvec, w_ada, b_ada):
    depth, d, six_d = w_ada.shape
    rows = cvec.shape[0]
    bn = 1024
    return pl.pallas_call(
        _ada_kernel,
        grid=(depth, six_d // bn),
        in_specs=[pl.BlockSpec((rows, d), lambda l, j: (0, 0)),
                  pl.BlockSpec((1, d, bn), lambda l, j: (l, 0, j)),
                  pl.BlockSpec((1, 1, bn), lambda l, j: (l, 0, j))],
        out_specs=pl.BlockSpec((1, rows, bn), lambda l, j: (l, 0, j)),
        out_shape=jax.ShapeDtypeStruct((depth, rows, six_d), F32),
        compiler_params=_cparams("parallel", "parallel"),
        name="adaln_mod",
    )(cvec, w_ada, b_ada.reshape(depth, 1, six_d))


def _inproj_kernel(x_ref, mod_ref, w_ref, cos_ref, sa_ref, sb_ref, qg_ref, kg_ref,
                   mqk_ref, mv_ref, mo_ref, gate_ref, dq_ref, dk_ref, dv_ref,
                   gq_ref, gk_ref, gv_ref):
    x = x_ref[0]
    mod = mod_ref[0, 0]
    xm = (x * (1.0 + mod[1:2]) + mod[0:1]).astype(BF16)
    cos, sa, sb = cos_ref[...], sa_ref[...], sb_ref[...]
    scale = ATTN_Q_SCALE

    def proj(lo, n):
        return _dot(xm, w_ref[:, lo:lo + n])

    def rope(a):
        return (a * cos + pltpu.roll(a, LANES - ROPE_HALF, 1) * sa
                + pltpu.roll(a, ROPE_HALF, 1) * sb)

    mqk_ref[0] = proj(C_MQK, 2 * MLSTM_W).astype(BF16)
    mv_ref[0] = proj(C_MV, MLSTM_W).astype(BF16)
    mo_ref[0] = proj(C_MO, MLSTM_W).astype(BF16)
    gate_ref[0] = proj(C_GATE, LANES)

    acc = proj(C_DQ, DIFF_W)
    for j in range(DIFF_W // LANES):
        dq_ref[0, :, j * LANES:(j + 1) * LANES] = (
            rope(acc[:, j * LANES:(j + 1) * LANES]) * scale).astype(BF16)
    acc = proj(C_DK, DIFF_W)
    for j in range(DIFF_W // LANES):
        dk_ref[0, :, j * LANES:(j + 1) * LANES] = rope(acc[:, j * LANES:(j + 1) * LANES]).astype(BF16)
    dv_ref[0] = proj(C_DV, DIFF_W).astype(BF16)

    head_mean = _head_mean_matrix(LANES)

    def rms(a, g):
        ms = _dot_split_lhs(a * a, head_mean)
        return a * lax.rsqrt(ms + LN_EPS) * g

    acc = proj(C_GQ, GQA_W)
    qg = qg_ref[...]
    for j in range(GQA_W // LANES):
        gq_ref[0, :, j * LANES:(j + 1) * LANES] = (
            rope(rms(acc[:, j * LANES:(j + 1) * LANES], qg)) * scale).astype(BF16)
    gk_ref[0] = rope(rms(proj(C_GK, GQA_KV_W), kg_ref[...])).astype(BF16)

    lane = lax.broadcasted_iota(jnp.int32, (1, 2 * LANES), 1)
    ones_col = jnp.logical_or(lane == HEAD_DIM, lane == LANES).astype(F32)
    gv_ref[0] = (proj(C_GV, 2 * LANES) + ones_col).astype(BF16)


def _inproj_call(xs, mod, w, tabs, qg, kg, n_lat):
    b, t, d = xs.shape
    n_tot = t // ROW_TILE
    tm = ROW_TILE
    row = lambda width: pl.BlockSpec((1, tm, width), lambda bi, i: (bi, i, 0))
    tab = pl.BlockSpec((tm, LANES), lambda bi, i: (i, 0))
    vec = pl.BlockSpec((1, LANES), lambda bi, i: (0, 0))
    widths = (2 * MLSTM_W, MLSTM_W, MLSTM_W, LANES, DIFF_W, DIFF_W, DIFF_W, GQA_W, GQA_KV_W, 2 * LANES)
    dtypes = (BF16, BF16, BF16, F32, BF16, BF16, BF16, BF16, BF16, BF16)
    return pl.pallas_call(
        _inproj_kernel,
        grid=(b, n_tot),
        in_specs=[row(d),
                  pl.BlockSpec((1, 1, 6, d), lambda bi, i: (bi, jnp.minimum(i // n_lat, 1), 0, 0)),
                  pl.BlockSpec((d, IN_COLS_PADDED), lambda bi, i: (0, 0)),
                  tab, tab, tab, vec, vec],
        out_specs=[row(wd) for wd in widths],
        out_shape=[jax.ShapeDtypeStruct((b, t, wd), dt) for wd, dt in zip(widths, dtypes)],
        compiler_params=_cparams("parallel", "parallel"),
        name="in_proj",
    )(xs, mod, w, *tabs, qg, kg)


def _mlstm_kernel(mqk_ref, mv_ref, mo_ref, gate_ref, cw_ref, cb_ref, gb_ref, ng_ref, out_ref,
                  qk_s, h_s, *, n_lat, n_tot):
    lc = ROW_TILE
    t_rows = n_tot * lc
    halo = BF16_SUBLANES

    cw = cw_ref[...]
    cb = cb_ref[...]
    lane_qk = lax.broadcasted_iota(jnp.int32, (1, 2 * MLSTM_W), 1)
    qscale = jnp.where(lane_qk < MLSTM_W, HEAD_DIM ** -0.5, 1.0)

    def conv_body(c, carry):
        r0 = pl.multiple_of(c * lc, lc)
        main = mqk_ref[0, pl.ds(r0, lc), :].astype(F32)
        ts = pl.multiple_of(jnp.maximum(r0 - halo, 0), halo)
        bs = pl.multiple_of(jnp.minimum(r0 + lc, t_rows - halo), halo)
        top_ok = jnp.logical_and(c != 0, c != n_lat).astype(F32)
        bot_ok = jnp.logical_and(c != n_lat - 1, c != n_tot - 1).astype(F32)
        top = mqk_ref[0, pl.ds(ts, halo), :].astype(F32) * top_ok
        bot = mqk_ref[0, pl.ds(bs, halo), :].astype(F32) * bot_ok
        win = jnp.concatenate([top, main, bot], axis=0)
        n = lc + 2 * halo
        y = win * cw[CONV_W // 2:CONV_W // 2 + 1]
        for j in range(CONV_W):
            if j != CONV_W // 2:
                y = y + pltpu.roll(win, (CONV_W // 2 - j) % n, 0) * cw[j:j + 1]
        y = y[halo:halo + lc] + cb
        qk_s[pl.ds(r0, lc), :] = (y * jax.nn.sigmoid(y) * qscale).astype(BF16)
        return carry

    lax.fori_loop(0, n_tot, conv_body, 0)

    row_i = lax.broadcasted_iota(jnp.int32, (lc, lc), 0)
    col_i = lax.broadcasted_iota(jnp.int32, (lc, lc), 1)
    masks_t = (row_i <= col_i, row_i >= col_i)
    tri_row = (masks_t[0].astype(BF16), masks_t[1].astype(BF16))
    tri_col = (tri_row[1], tri_row[0])
    sub = lax.broadcasted_iota(jnp.int32, (LANES, lc), 0)
    sels = (sub < HEAD_DIM, sub >= HEAD_DIM)
    one_rows = ((sub == HEAD_DIM).astype(BF16), (sub == 0).astype(BF16))
    den_row = (HEAD_DIM, 0)
    gb = gb_ref[...]
    h_s[...] = jnp.zeros_like(h_s)
    n_streams = 2 * MLSTM_HEADS

    def step(j, carry):
        cns, ms = carry
        new_cns, new_ms = list(cns), list(ms)
        for d in range(2):
            c = (j + n_lat) % n_tot if d == 0 else n_tot - 1 - j
            r0 = pl.multiple_of(c * lc, lc)
            gates = gate_ref[0, pl.ds(r0, lc), :] + gb
            gates_t = gates.T
            cum = _dot_split_rhs(tri_col[d], jax.nn.log_sigmoid(gates))
            cum_t = _dot_split_lhs(jax.nn.log_sigmoid(gates_t), tri_row[d])
            qk = qk_s[pl.ds(r0, lc), :]
            vv = mv_ref[0, pl.ds(r0, lc), :]
            last = lc - 1 if d == 0 else 0
            for p in range(MLSTM_HEADS // 2):
                q_t = qk[:, p * LANES:(p + 1) * LANES].T
                k128 = qk[:, MLSTM_W + p * LANES:MLSTM_W + (p + 1) * LANES]
                v_t = vv[:, p * LANES:(p + 1) * LANES].T
                halves = []
                for e in range(2):
                    hh = 2 * p + e
                    sidx = d * MLSTM_HEADS + hh
                    ci = d * MLSTM_HEADS + hh
                    cf = 2 * MLSTM_HEADS + d * MLSTM_HEADS + hh
                    qm_t = jnp.where(sels[e], q_t, jnp.zeros_like(q_t))
                    v1_t = jnp.where(sels[e], v_t, one_rows[e])
                    i_row = gates_t[ci:ci + 1, :]
                    a_row = cum_t[cf:cf + 1, :]
                    r_col = gates[:, ci:ci + 1] - cum[:, cf:cf + 1]
                    m = ms[sidx]
                    cn_t = cns[sidx]
                    dm = jnp.where(masks_t[d], a_row + r_col, -jnp.inf)
                    inter = a_row + m
                    m_t = jnp.maximum(jnp.max(dm, axis=0, keepdims=True), inter)
                    w_prev = jnp.exp(inter - m_t)
                    sw = (_dot(k128, qm_t) * jnp.exp(dm - m_t)).astype(BF16)
                    nd = _dot(v1_t, sw) + w_prev * _dot(cn_t.astype(BF16), qm_t)
                    den = nd[den_row[e]:den_row[e] + 1, :]
                    halves.append(nd * (1.0 / jnp.maximum(jnp.abs(den), jnp.exp(-m_t))))
                    b_last = cum_t[cf:cf + 1, last:last + 1]
                    logu = b_last - a_row + i_row
                    m_new = jnp.maximum(b_last + m, jnp.max(logu, axis=1, keepdims=True))
                    vu = (v1_t.astype(F32) * jnp.exp(logu - m_new)).astype(BF16)
                    new_cns[sidx] = jnp.exp(b_last + m - m_new) * cn_t + _dot(vu, k128)
                    new_ms[sidx] = m_new
                h_pair = jnp.where(sels[0], halves[0], halves[1]).T
                h_s[pl.ds(r0, lc), p * LANES:(p + 1) * LANES] += h_pair
        return tuple(new_cns), tuple(new_ms)

    init = (tuple(jnp.zeros((LANES, LANES), F32) for _ in range(n_streams)),
            tuple(jnp.zeros((1, 1), F32) for _ in range(n_streams)))
    lax.fori_loop(0, n_tot, step, init)

    head_mean = _head_mean_matrix(MLSTM_W)
    ng = ng_ref[...]

    def out_body(c, carry):
        r0 = pl.multiple_of(c * lc, lc)
        hb = h_s[pl.ds(r0, lc), :]
        hc = hb - _dot_split_lhs(hb, head_mean)
        var = _dot_split_lhs(hc * hc, head_mean)
        o = mo_ref[0, pl.ds(r0, lc), :].astype(F32)
        out_ref[0, pl.ds(r0, lc), :] = (hc * lax.rsqrt(var + LN_EPS) * ng * jax.nn.sigmoid(o)).astype(BF16)
        return carry

    lax.fori_loop(0, n_tot, out_body, 0)


def _mlstm_call(mqk, mv, mo, gate, cw, cb, gb, ng, n_lat):
    b, t, _ = mqk.shape
    n_tot = t // ROW_TILE
    seq = lambda width: pl.BlockSpec((1, t, width), lambda bi: (bi, 0, 0))
    full = lambda a: pl.BlockSpec(a.shape, lambda bi: (0, 0))
    return pl.pallas_call(
        functools.partial(_mlstm_kernel, n_lat=n_lat, n_tot=n_tot),
        grid=(b,),
        in_specs=[seq(2 * MLSTM_W), seq(MLSTM_W), seq(MLSTM_W), seq(LANES),
                  full(cw), full(cb), full(gb), full(ng)],
        out_specs=seq(MLSTM_W),
        out_shape=jax.ShapeDtypeStruct((b, t, MLSTM_W), BF16),
        scratch_shapes=[pltpu.VMEM((t, 2 * MLSTM_W), BF16), pltpu.VMEM((t, MLSTM_W), F32)],
        compiler_params=_cparams("parallel"),
        name="mlstm",
    )(mqk, mv, mo, gate, cw, cb, gb, ng)


def _key_blocks(first, n_rows, tk):
    blocks, r = [], first
    while r < first + n_rows:
        size = min(tk, first + n_rows - r)
        blocks.append((r, size))
        r += size
    return blocks


def _attn_pair_tile(qms, k_ref, v1_fn, blocks):
    ms, accs = [None, None], [None, None]
    for start, size in blocks:
        k = k_ref[0, pl.ds(start, size), :]
        for e in range(2):
            s = _dot_nt(qms[e], k)
            row_max = jnp.max(s, axis=1, keepdims=True)
            m_new = row_max if ms[e] is None else jnp.maximum(ms[e], row_max)
            pv = _dot(jnp.exp2(s - m_new).astype(BF16), v1_fn(e, start, size))
            accs[e] = pv if accs[e] is None else jnp.exp2(ms[e] - m_new) * accs[e] + pv
            ms[e] = m_new
    return accs


def _for_query_tiles(n_lat_rows, n_ctx_rows, tq, tk, tile_fn):
    lat_blocks = _key_blocks(0, n_lat_rows, tk) + _key_blocks(n_lat_rows, n_ctx_rows, tk)
    ctx_blocks = _key_blocks(n_lat_rows, n_ctx_rows, tk)

    def lat_body(i, carry):
        tile_fn(pl.multiple_of(i * tq, tq), tq, lat_blocks)
        return carry

    lax.fori_loop(0, n_lat_rows // tq, lat_body, 0)
    tq_ctx = min(tq, n_ctx_rows)
    for i in range(n_ctx_rows // tq_ctx):
        tile_fn(n_lat_rows + i * tq_ctx, tq_ctx, ctx_blocks)


def _diff_attn_kernel(q_ref, k_ref, v_ref, lam_ref, g_ref, o_ref, *, n_lat_rows, tq, tk, lam_init):
    dv = 2 * HEAD_DIM
    n_ctx_rows = q_ref.shape[1] - n_lat_rows
    lane = lax.broadcasted_iota(jnp.int32, (1, LANES), 1)
    ones_blk = jnp.broadcast_to((lane == 0).astype(BF16), (tk, LANES))
    lv = lam_ref[0]
    lam = (jnp.exp(jnp.sum(lv[0:1] * lv[1:2], axis=1, keepdims=True))
           - jnp.exp(jnp.sum(lv[2:3] * lv[3:4], axis=1, keepdims=True)) + lam_init)
    gain = g_ref[...] * (1.0 - lam_init)

    def v1_fn(e, start, size):
        return jnp.concatenate([v_ref[0, pl.ds(start, size), :], ones_blk[:size]], axis=1)

    def tile_fn(r0, rows, blocks):
        q = q_ref[0, pl.ds(r0, rows), :]
        zero = jnp.zeros_like(q)
        qms = (jnp.where(lane < HEAD_DIM, q, zero), jnp.where(lane >= HEAD_DIM, q, zero))
        a1, a2 = _attn_pair_tile(qms, k_ref, v1_fn, blocks)
        o = a1[:, :dv] * (1.0 / a1[:, dv:dv + 1]) - lam * (a2[:, :dv] * (1.0 / a2[:, dv:dv + 1]))
        ms = jnp.mean(o * o, axis=1, keepdims=True)
        o_ref[0, pl.ds(r0, rows), :] = (o * lax.rsqrt(ms + LN_EPS) * gain).astype(BF16)

    _for_query_tiles(n_lat_rows, n_ctx_rows, tq, tk, tile_fn)


def _diff_attn_call(dq, dk, dv, lam_vecs, g, n_lat, lam_init):
    b, t, _ = dq.shape
    seq = pl.BlockSpec((1, t, LANES), lambda bi, h: (bi, 0, h))
    return pl.pallas_call(
        functools.partial(_diff_attn_kernel, n_lat_rows=n_lat * ROW_TILE, tq=ATTN_TQ, tk=DIFF_TK,
                          lam_init=lam_init),
        grid=(b, DIFF_HEADS),
        in_specs=[seq, seq, seq,
                  pl.BlockSpec((1, 4, HEAD_DIM), lambda bi, h: (0, 0, 0)),
                  pl.BlockSpec((1, LANES), lambda bi, h: (0, 0))],
        out_specs=seq,
        out_shape=jax.ShapeDtypeStruct((b, t, DIFF_W), BF16),
        compiler_params=_cparams("parallel", "parallel"),
        name="diff_attn",
    )(dq, dk, dv, lam_vecs, g)


def _gqa_attn_kernel(q_ref, k_ref, v_ref, o_ref, *, n_lat_rows, tq, tk):
    n_ctx_rows = q_ref.shape[1] - n_lat_rows
    lane = lax.broadcasted_iota(jnp.int32, (1, LANES), 1)
    first_half = lane < HEAD_DIM
    den_lane = (HEAD_DIM, 0)

    def v1_fn(e, start, size):
        return v_ref[0, pl.ds(start, size), e * LANES:(e + 1) * LANES]

    def tile_fn(r0, rows, blocks):
        q = q_ref[0, pl.ds(r0, rows), :]
        zero = jnp.zeros_like(q)
        qms = (jnp.where(first_half, q, zero), jnp.where(first_half, zero, q))
        accs = _attn_pair_tile(qms, k_ref, v1_fn, blocks)
        halves = [acc * (1.0 / acc[:, den_lane[e]:den_lane[e] + 1]) for e, acc in enumerate(accs)]
        o_ref[0, pl.ds(r0, rows), :] = jnp.where(first_half, halves[0], halves[1]).astype(BF16)

    _for_query_tiles(n_lat_rows, n_ctx_rows, tq, tk, tile_fn)


def _gqa_attn_call(gq, gk, gv, n_lat):
    b, t, _ = gq.shape
    pair = pl.BlockSpec((1, t, LANES), lambda bi, p: (bi, 0, p))
    return pl.pallas_call(
        functools.partial(_gqa_attn_kernel, n_lat_rows=n_lat * ROW_TILE, tq=ATTN_TQ, tk=GQA_TK),
        grid=(b, GQA_W // LANES),
        in_specs=[pair,
                  pl.BlockSpec((1, t, GQA_KV_W), lambda bi, p: (bi, 0, 0)),
                  pl.BlockSpec((1, t, 2 * LANES), lambda bi, p: (bi, 0, 0))],
        out_specs=pair,
        out_shape=jax.ShapeDtypeStruct((b, t, GQA_W), BF16),
        compiler_params=_cparams("parallel", "arbitrary"),
        name="gqa_attn",
    )(gq, gk, gv)


def _outproj_kernel(a_ref, bd_ref, cg_ref, x_ref, mod_ref, w_ref, g_ref, b_ref, o_ref, *, alpha):
    y = (_dot(a_ref[0], w_ref[0:MLSTM_W, :])
         + _dot(bd_ref[0], w_ref[MLSTM_W:MLSTM_W + DIFF_W, :])
         + _dot(cg_ref[0], w_ref[MLSTM_W + DIFF_W:, :]))
    mod = mod_ref[0, 0]
    o_ref[0] = _layer_norm(alpha * x_ref[0] + mod[2:3] * y, g_ref[...], b_ref[...])


def _outproj_call(a, bd, cg, xs, mod, w, g, bias, n_lat, n_rows, alpha):
    b, _, d = xs.shape
    tm = ROW_TILE
    row = lambda width: pl.BlockSpec((1, tm, width), lambda bi, i: (bi, i, 0))
    vec = pl.BlockSpec((1, d), lambda bi, i: (0, 0))
    return pl.pallas_call(
        functools.partial(_outproj_kernel, alpha=alpha),
        grid=(b, n_rows // tm),
        in_specs=[row(MLSTM_W), row(DIFF_W), row(GQA_W), row(d),
                  pl.BlockSpec((1, 1, 6, d), lambda bi, i: (bi, jnp.minimum(i // n_lat, 1), 0, 0)),
                  pl.BlockSpec(w.shape, lambda bi, i: (0, 0)), vec, vec],
        out_specs=row(d),
        out_shape=jax.ShapeDtypeStruct((b, n_rows, d), F32),
        compiler_params=_cparams("parallel", "parallel"),
        name="out_proj_ln",
    )(a, bd, cg, xs, mod, w, g, bias)


def _ffn_kernel(x_ref, mod_ref, wi_ref, wo_ref, g_ref, b_ref, o_ref, *, alpha, d_ff, chunk):
    x = x_ref[0]
    mod = mod_ref[0, 0]
    xm = (x * (1.0 + mod[4:5]) + mod[3:4]).astype(BF16)
    acc = jnp.zeros(x.shape, F32)
    for c in range(d_ff // chunk):
        gate = _dot(xm, wi_ref[:, c * chunk:(c + 1) * chunk])
        up = _dot(xm, wi_ref[:, d_ff + c * chunk:d_ff + (c + 1) * chunk])
        act = (gate * jax.nn.sigmoid(gate) * up).astype(BF16)
        acc = acc + _dot(act, wo_ref[c * chunk:(c + 1) * chunk, :])
    o_ref[0] = _layer_norm(alpha * x + mod[5:6] * acc, g_ref[...], b_ref[...])


def _ffn_call(xs, mod, wi, wo, g, bias, n_lat, alpha):
    b, n_rows, d = xs.shape
    d_ff = wo.shape[0]
    tm = ROW_TILE
    row = pl.BlockSpec((1, tm, d), lambda bi, i: (bi, i, 0))
    vec = pl.BlockSpec((1, d), lambda bi, i: (0, 0))
    return pl.pallas_call(
        functools.partial(_ffn_kernel, alpha=alpha, d_ff=d_ff, chunk=2 * LANES),
        grid=(b, n_rows // tm),
        in_specs=[row,
                  pl.BlockSpec((1, 1, 6, d), lambda bi, i: (bi, jnp.minimum(i // n_lat, 1), 0, 0)),
                  pl.BlockSpec(wi.shape, lambda bi, i: (0, 0)),
                  pl.BlockSpec(wo.shape, lambda bi, i: (0, 0)), vec, vec],
        out_specs=row,
        out_shape=jax.ShapeDtypeStruct((b, n_rows, d), F32),
        compiler_params=_cparams("parallel", "parallel"),
        name="ffn_ln",
    )(xs, mod, wi, wo, g, bias)


def _prep_w_in(w):
    zeros = lambda n: jnp.zeros((w.shape[0], n), w.dtype)
    o_gate = 4 * MLSTM_W
    o_dq = o_gate + 4 * MLSTM_HEADS
    o_gq = o_dq + 3 * DIFF_W
    o_gk = o_gq + GQA_W
    o_gv = o_gk + GQA_KV_W
    gq = w[:, o_gq:o_gk].reshape(-1, 2, 2, HEAD_DIM).transpose(0, 2, 1, 3).reshape(-1, GQA_W)
    gv = w[:, o_gv:o_gv + GQA_KV_W]
    cols = [w[:, :o_gate], w[:, o_dq:o_gq], gq, w[:, o_gk:o_gv],
            gv[:, :HEAD_DIM], zeros(LANES), gv[:, HEAD_DIM:],
            w[:, o_gate:o_dq], zeros(LANES - 4 * MLSTM_HEADS)]
    out = jnp.concatenate(cols, axis=1)
    assert out.shape[1] == IN_COLS_PADDED
    return out.astype(BF16)


def _prep_w_out(w):
    o_c = MLSTM_W + DIFF_W
    gq = w[o_c:].reshape(2, 2, HEAD_DIM, -1).transpose(1, 0, 2, 3).reshape(GQA_W, -1)
    return jnp.concatenate([w[:o_c], gq], axis=0).astype(BF16)


def _rope_tables(s, n_ctx_rows):
    pos = jnp.arange(s, dtype=jnp.int32)
    row = (pos // GRID_W).astype(F32)
    col = (pos % GRID_W).astype(F32)
    n_freq = HEAD_DIM // 4
    inv = ROPE_THETA ** (-jnp.arange(n_freq, dtype=F32) / n_freq)
    ar = row[:, None] * inv
    ac = col[:, None] * inv
    ang = jnp.concatenate([ar, ar, ac, ac], axis=-1)
    ang = jnp.concatenate([ang, ang], axis=-1)
    first = (jnp.arange(LANES) % (2 * ROPE_HALF)) < ROPE_HALF
    cos, sin = jnp.cos(ang), jnp.sin(ang)
    sa = jnp.where(first, -sin, 0.0)
    sb = jnp.where(first, 0.0, sin)
    pad = lambda a, v: jnp.concatenate([a, jnp.full((n_ctx_rows, LANES), v, F32)], axis=0)
    return pad(cos, 1.0), pad(sa, 0.0), pad(sb, 0.0)


def kernel(x, c, ctx, c_ctx, w_ada, b_ada, w_in, mlstm_conv_w, mlstm_conv_b, mlstm_gate_b, mlstm_norm_g, diff_lambda, diff_norm_g, gqa_q_norm_g, gqa_k_norm_g, w_out, ln1_g, ln1_b, w_ffn_in, w_ffn_out, ln2_g, ln2_b):
    b, s, d = x.shape
    n_ctx_rows = ctx.shape[1]
    depth = w_in.shape[0]
    assert s % ROW_TILE == 0 and n_ctx_rows % ROW_TILE == 0 and s % GRID_W == 0
    n_lat = s // ROW_TILE
    t = s + n_ctx_rows
    alpha = (2 * depth) ** 0.25

    ada_rows = -(-(b + 1) // 8) * 8
    cvec = jnp.concatenate([c, c_ctx[None, :], jnp.zeros((ada_rows - b - 1, d), F32)], axis=0)
    mod_all = _ada_call(cvec, w_ada, b_ada)
    mod_lat = mod_all[:, :b].reshape(depth, b, 1, 6, d)
    mod_ctx = jnp.broadcast_to(mod_all[:, b].reshape(depth, 1, 1, 6, d), (depth, b, 1, 6, d))
    mod_all = jnp.concatenate([mod_lat, mod_ctx], axis=2)

    tabs = _rope_tables(s, n_ctx_rows)
    tile2 = lambda v, n: jnp.tile(v, n).reshape(1, -1)
    xs = jnp.concatenate([x, ctx], axis=1)

    for l in range(depth):
        last = l == depth - 1
        lam_init = 0.8 - 0.6 * math.exp(-0.3 * l)
        mod = mod_all[l]
        (mqk, mv, mo, gate, dq, dk, dv, gq, gk, gv) = _inproj_call(
            xs, mod, _prep_w_in(w_in[l]), tabs,
            tile2(gqa_q_norm_g[l], LANES // HEAD_DIM), tile2(gqa_k_norm_g[l], LANES // HEAD_DIM), n_lat)
        cw = jnp.concatenate([mlstm_conv_w[l], jnp.zeros((8 - CONV_W, 2 * MLSTM_W), F32)], axis=0)
        gb = jnp.concatenate([mlstm_gate_b[l], jnp.zeros((LANES - 4 * MLSTM_HEADS,), F32)]).reshape(1, LANES)
        a = _mlstm_call(mqk, mv, mo, gate, cw, mlstm_conv_b[l].reshape(1, -1), gb,
                        mlstm_norm_g[l].reshape(1, -1), n_lat)
        bd = _diff_attn_call(dq, dk, dv, diff_lambda[l][None], diff_norm_g[l].reshape(1, -1), n_lat, lam_init)
        cg = _gqa_attn_call(gq, gk, gv, n_lat)
        n_rows = s if last else t
        xs = _outproj_call(a, bd, cg, xs, mod, _prep_w_out(w_out[l]),
                           ln1_g[l].reshape(1, -1), ln1_b[l].reshape(1, -1), n_lat, n_rows, alpha)
        xs = _ffn_call(xs, mod, w_ffn_in[l].astype(BF16), w_ffn_out[l].astype(BF16),
                       ln2_g[l].reshape(1, -1), ln2_b[l].reshape(1, -1), n_lat, alpha)
    return xs
```

```python
import functools
import math

import jax
import jax.numpy as jnp
from jax import lax
from jax.experimental import pallas as pl
from jax.experimental.pallas import tpu as pltpu

F32 = jnp.float32
BF16 = jnp.bfloat16

HEAD_DIM = 64
MLSTM_HEADS = 4
DIFF_HEADS = 4
GQA_HEADS = 4
GQA_KV_HEADS = 2
GRID_W = 64
CONV_W = 5
ROPE_THETA = 10000.0
LN_EPS = 1e-5

MLSTM_W = MLSTM_HEADS * HEAD_DIM
DIFF_W = DIFF_HEADS * 2 * HEAD_DIM
GQA_W = GQA_HEADS * HEAD_DIM
GQA_KV_W = GQA_KV_HEADS * HEAD_DIM

LANES = 128
BF16_SUBLANES = 16
VMEM_LIMIT_BYTES = 56 * 1024 * 1024

ROW_TILE = 256
ROPE_HALF = HEAD_DIM // 4
DIFF_TILES = (512, 1024, 1)
GQA_TILES = (256, 4096, 2)
ATTN_Q_SCALE = HEAD_DIM ** -0.5 * math.log2(math.e)

C_MQK, C_MV, C_MO = 0, 512, 768
C_DQ, C_DK, C_DV = 1024, 1536, 2048
C_GQ, C_GK, C_GV, C_GATE = 2560, 2816, 2944, 3200
IN_COLS_PADDED = 3328


def _cparams(*sem):
    return pltpu.CompilerParams(dimension_semantics=sem, vmem_limit_bytes=VMEM_LIMIT_BYTES)


def _dot(a, b):
    return jnp.dot(a, b, preferred_element_type=F32)


def _dot_nt(a, b):
    return lax.dot_general(a, b, (((1,), (1,)), ((), ())), preferred_element_type=F32)


def _split_bf16(a):
    hi = a.astype(BF16)
    lo = (a - hi.astype(F32)).astype(BF16)
    return hi, lo


def _dot_split_lhs(a, m):
    hi, lo = _split_bf16(a)
    return _dot(hi, m) + _dot(lo, m)


def _dot_split_rhs(m, a):
    hi, lo = _split_bf16(a)
    return _dot(m, hi) + _dot(m, lo)


def _head_mean_matrix(width):
    r = lax.broadcasted_iota(jnp.int32, (width, width), 0) // HEAD_DIM
    c = lax.broadcasted_iota(jnp.int32, (width, width), 1) // HEAD_DIM
    return jnp.where(r == c, 1.0 / HEAD_DIM, 0.0).astype(BF16)


def _layer_norm(z, g, b):
    mu = jnp.mean(z, axis=-1, keepdims=True)
    zc = z - mu
    var = jnp.mean(zc * zc, axis=-1, keepdims=True)
    return zc * lax.rsqrt(var + LN_EPS) * g + b


def _ada_kernel(c_ref, w_ref, b_ref, o_ref):
    c = c_ref[...]
    o_ref[0] = _dot(c * jax.nn.sigmoid(c), w_ref[0]) + b_ref[0]


def _ada_call(cvec, w_ada, b_ada):
    depth, d, six_d = w_ada.shape
    rows = cvec.shape[0]
    bn = 1024
    return pl.pallas_call(
        _ada_kernel,
        grid=(depth, six_d // bn),
        in_specs=[pl.BlockSpec((rows, d), lambda l, j: (0, 0)),
                  pl.BlockSpec((1, d, bn), lambda l, j: (l, 0, j)),
                  pl.BlockSpec((1, 1, bn), lambda l, j: (l, 0, j))],
        out_specs=pl.BlockSpec((1, rows, bn), lambda l, j: (l, 0, j)),
        out_shape=jax.ShapeDtypeStruct((depth, rows, six_d), F32),
        compiler_params=_cparams("parallel", "parallel"),
        name="adaln_mod",
    )(cvec, w_ada, b_ada.reshape(depth, 1, six_d))


def _inproj_kernel(x_ref, mod_ref, w_ref, cos_ref, sa_ref, sb_ref, qg_ref, kg_ref,
                   mqk_ref, mv_ref, mo_ref, gate_ref, dq_ref, dk_ref, dv_ref,
                   gq_ref, gk_ref, gv_ref):
    x = x_ref[0]
    mod = mod_ref[0, 0]
    xm = (x * (1.0 + mod[1:2]) + mod[0:1]).astype(BF16)
    cos, sa, sb = cos_ref[...], sa_ref[...], sb_ref[...]
    scale = ATTN_Q_SCALE

    def proj(lo, n):
        return _dot(xm, w_ref[:, lo:lo + n])

    def rope(a):
        return (a * cos + pltpu.roll(a, LANES - ROPE_HALF, 1) * sa
                + pltpu.roll(a, ROPE_HALF, 1) * sb)

    mqk_ref[0] = proj(C_MQK, 2 * MLSTM_W).astype(BF16)
    mv_ref[0] = proj(C_MV, MLSTM_W).astype(BF16)
    mo_ref[0] = proj(C_MO, MLSTM_W).astype(BF16)
    gate_ref[0] = proj(C_GATE, LANES)

    acc = proj(C_DQ, DIFF_W)
    for j in range(DIFF_W // LANES):
        dq_ref[0, :, j * LANES:(j + 1) * LANES] = (
            rope(acc[:, j * LANES:(j + 1) * LANES]) * scale).astype(BF16)
    acc = proj(C_DK, DIFF_W)
    for j in range(DIFF_W // LANES):
        dk_ref[0, :, j * LANES:(j + 1) * LANES] = rope(acc[:, j * LANES:(j + 1) * LANES]).astype(BF16)
    dv_ref[0] = proj(C_DV, DIFF_W).astype(BF16)

    head_mean = _head_mean_matrix(LANES)

    def rms(a, g):
        ms = _dot_split_lhs(a * a, head_mean)
        return a * lax.rsqrt(ms + LN_EPS) * g

    acc = proj(C_GQ, GQA_W)
    qg = qg_ref[...]
    for j in range(GQA_W // LANES):
        gq_ref[0, :, j * LANES:(j + 1) * LANES] = (
            rope(rms(acc[:, j * LANES:(j + 1) * LANES], qg)) * scale).astype(BF16)
    gk_ref[0] = rope(rms(proj(C_GK, GQA_KV_W), kg_ref[...])).astype(BF16)

    lane = lax.broadcasted_iota(jnp.int32, (1, 2 * LANES), 1)
    ones_col = jnp.logical_or(lane == HEAD_DIM, lane == LANES).astype(F32)
    gv_ref[0] = (proj(C_GV, 2 * LANES) + ones_col).astype(BF16)


def _inproj_call(xs, mod, w, tabs, qg, kg, n_lat):
    b, t, d = xs.shape
    n_tot = t // ROW_TILE
    tm = ROW_TILE
    row = lambda width: pl.BlockSpec((1, tm, width), lambda bi, i: (bi, i, 0))
    tab = pl.BlockSpec((tm, LANES), lambda bi, i: (i, 0))
    vec = pl.BlockSpec((1, LANES), lambda bi, i: (0, 0))
    widths = (2 * MLSTM_W, MLSTM_W, MLSTM_W, LANES, DIFF_W, DIFF_W, DIFF_W, GQA_W, GQA_KV_W, 2 * LANES)
    dtypes = (BF16, BF16, BF16, F32, BF16, BF16, BF16, BF16, BF16, BF16)
    return pl.pallas_call(
        _inproj_kernel,
        grid=(b, n_tot),
        in_specs=[row(d),
                  pl.BlockSpec((1, 1, 6, d), lambda bi, i: (bi, jnp.minimum(i // n_lat, 1), 0, 0)),
                  pl.BlockSpec((d, IN_COLS_PADDED), lambda bi, i: (0, 0)),
                  tab, tab, tab, vec, vec],
        out_specs=[row(wd) for wd in widths],
        out_shape=[jax.ShapeDtypeStruct((b, t, wd), dt) for wd, dt in zip(widths, dtypes)],
        compiler_params=_cparams("parallel", "parallel"),
        name="in_proj",
    )(xs, mod, w, *tabs, qg, kg)


def _mlstm_kernel(mqk_ref, mv_ref, mo_ref, gate_ref, cw_ref, cb_ref, gb_ref, ng_ref, out_ref,
                  qk_s, h_s, *, n_lat, n_tot):
    lc = ROW_TILE
    t_rows = n_tot * lc
    halo = BF16_SUBLANES

    cw = cw_ref[...]
    cb = cb_ref[...]
    lane_qk = lax.broadcasted_iota(jnp.int32, (1, 2 * MLSTM_W), 1)
    qscale = jnp.where(lane_qk < MLSTM_W, HEAD_DIM ** -0.5, 1.0)

    def conv_body(c, carry):
        r0 = pl.multiple_of(c * lc, lc)
        main = mqk_ref[0, pl.ds(r0, lc), :].astype(F32)
        ts = pl.multiple_of(jnp.maximum(r0 - halo, 0), halo)
        bs = pl.multiple_of(jnp.minimum(r0 + lc, t_rows - halo), halo)
        top_ok = jnp.logical_and(c != 0, c != n_lat).astype(F32)
        bot_ok = jnp.logical_and(c != n_lat - 1, c != n_tot - 1).astype(F32)
        top = mqk_ref[0, pl.ds(ts, halo), :].astype(F32) * top_ok
        bot = mqk_ref[0, pl.ds(bs, halo), :].astype(F32) * bot_ok
        win = jnp.concatenate([top, main, bot], axis=0)
        n = lc + 2 * halo
        y = win * cw[CONV_W // 2:CONV_W // 2 + 1]
        for j in range(CONV_W):
            if j != CONV_W // 2:
                y = y + pltpu.roll(win, (CONV_W // 2 - j) % n, 0) * cw[j:j + 1]
        y = y[halo:halo + lc] + cb
        qk_s[pl.ds(r0, lc), :] = (y * jax.nn.sigmoid(y) * qscale).astype(BF16)
        return carry

    lax.fori_loop(0, n_tot, conv_body, 0)

    row_i = lax.broadcasted_iota(jnp.int32, (lc, lc), 0)
    col_i = lax.broadcasted_iota(jnp.int32, (lc, lc), 1)
    masks_t = (row_i <= col_i, row_i >= col_i)
    tri_row = (masks_t[0].astype(BF16), masks_t[1].astype(BF16))
    tri_col = (tri_row[1], tri_row[0])
    sub = lax.broadcasted_iota(jnp.int32, (LANES, lc), 0)
    sels = (sub < HEAD_DIM, sub >= HEAD_DIM)
    one_rows = ((sub == HEAD_DIM).astype(BF16), (sub == 0).astype(BF16))
    den_row = (HEAD_DIM, 0)
    gb = gb_ref[...]
    h_s[...] = jnp.zeros_like(h_s)
    n_streams = 2 * MLSTM_HEADS

    def step(j, carry):
        cns, ms = carry
        new_cns, new_ms = list(cns), list(ms)
        for d in range(2):
            c = (j + n_lat) % n_tot if d == 0 else n_tot - 1 - j
            r0 = pl.multiple_of(c * lc, lc)
            gates = gate_ref[0, pl.ds(r0, lc), :] + gb
            gates_t = gates.T
            cum = _dot_split_rhs(tri_col[d], jax.nn.log_sigmoid(gates))
            cum_t = _dot_split_lhs(jax.nn.log_sigmoid(gates_t), tri_row[d])
            qk = qk_s[pl.ds(r0, lc), :]
            vv = mv_ref[0, pl.ds(r0, lc), :]
            last = lc - 1 if d == 0 else 0
            for p in range(MLSTM_HEADS // 2):
                q_t = qk[:, p * LANES:(p + 1) * LANES].T
                k128 = qk[:, MLSTM_W + p * LANES:MLSTM_W + (p + 1) * LANES]
                v_t = vv[:, p * LANES:(p + 1) * LANES].T
                halves = []
                for e in range(2):
                    hh = 2 * p + e
                    sidx = d * MLSTM_HEADS + hh
                    ci = d * MLSTM_HEADS + hh
                    cf = 2 * MLSTM_HEADS + d * MLSTM_HEADS + hh
                    qm_t = jnp.where(sels[e], q_t, jnp.zeros_like(q_t))
                    v1_t = jnp.where(sels[e], v_t, one_rows[e])
                    i_row = gates_t[ci:ci + 1, :]
                    a_row = cum_t[cf:cf + 1, :]
                    r_col = gates[:, ci:ci + 1] - cum[:, cf:cf + 1]
                    m = ms[sidx]
                    cn_t = cns[sidx]
                    dm = jnp.where(masks_t[d], a_row + r_col, -jnp.inf)
                    inter = a_row + m
                    m_t = jnp.maximum(jnp.max(dm, axis=0, keepdims=True), inter)
                    w_prev = jnp.exp(inter - m_t)
                    sw = (_dot(k128, qm_t) * jnp.exp(dm - m_t)).astype(BF16)
                    nd = _dot(v1_t, sw) + w_prev * _dot(cn_t.astype(BF16), qm_t)
                    den = nd[den_row[e]:den_row[e] + 1, :]
                    halves.append(nd * (1.0 / jnp.maximum(jnp.abs(den), jnp.exp(-m_t))))
                    b_last = cum_t[cf:cf + 1, last:last + 1]
                    logu = b_last - a_row + i_row
                    m_new = jnp.maximum(b_last + m, jnp.max(logu, axis=1, keepdims=True))
                    vu = (v1_t.astype(F32) * jnp.exp(logu - m_new)).astype(BF16)
                    new_cns[sidx] = jnp.exp(b_last + m - m_new) * cn_t + _dot(vu, k128)
                    new_ms[sidx] = m_new
                h_pair = jnp.where(sels[0], halves[0], halves[1]).T
                h_s[pl.ds(r0, lc), p * LANES:(p + 1) * LANES] += h_pair
        return tuple(new_cns), tuple(new_ms)

    init = (tuple(jnp.zeros((LANES, LANES), F32) for _ in range(n_streams)),
            tuple(jnp.zeros((1, 1), F32) for _ in range(n_streams)))
    lax.fori_loop(0, n_tot, step, init)

    head_mean = _head_mean_matrix(MLSTM_W)
    ng = ng_ref[...]

    def out_body(c, carry):
        r0 = pl.multiple_of(c * lc, lc)
        hb = h_s[pl.ds(r0, lc), :]
        hc = hb - _dot_split_lhs(hb, head_mean)
        var = _dot_split_lhs(hc * hc, head_mean)
        o = mo_ref[0, pl.ds(r0, lc), :].astype(F32)
        out_ref[0, pl.ds(r0, lc), :] = (hc * lax.rsqrt(var + LN_EPS) * ng * jax.nn.sigmoid(o)).astype(BF16)
        return carry

    lax.fori_loop(0, n_tot, out_body, 0)


def _mlstm_call(mqk, mv, mo, gate, cw, cb, gb, ng, n_lat):
    b, t, _ = mqk.shape
    n_tot = t // ROW_TILE
    seq = lambda width: pl.BlockSpec((1, t, width), lambda bi: (bi, 0, 0))
    full = lambda a: pl.BlockSpec(a.shape, lambda bi: (0, 0))
    return pl.pallas_call(
        functools.partial(_mlstm_kernel, n_lat=n_lat, n_tot=n_tot),
        grid=(b,),
        in_specs=[seq(2 * MLSTM_W), seq(MLSTM_W), seq(MLSTM_W), seq(LANES),
                  full(cw), full(cb), full(gb), full(ng)],
        out_specs=seq(MLSTM_W),
        out_shape=jax.ShapeDtypeStruct((b, t, MLSTM_W), BF16),
        scratch_shapes=[pltpu.VMEM((t, 2 * MLSTM_W), BF16), pltpu.VMEM((t, MLSTM_W), F32)],
        compiler_params=_cparams("parallel"),
        name="mlstm",
    )(mqk, mv, mo, gate, cw, cb, gb, ng)


def _key_blocks(first, n_rows, tk):
    blocks, r = [], first
    while r < first + n_rows:
        size = min(tk, first + n_rows - r)
        blocks.append((r, size))
        r += size
    return blocks


def _attn_pair_tile(qms, k_ref, v1_fn, blocks):
    ms, accs = [None, None], [None, None]
    for start, size in blocks:
        k = k_ref[0, pl.ds(start, size), :]
        for e in range(2):
            s = _dot_nt(qms[e], k)
            row_max = jnp.max(s, axis=1, keepdims=True)
            m_new = row_max if ms[e] is None else jnp.maximum(ms[e], row_max)
            pv = _dot(jnp.exp2(s - m_new).astype(BF16), v1_fn(e, start, size))
            accs[e] = pv if accs[e] is None else jnp.exp2(ms[e] - m_new) * accs[e] + pv
            ms[e] = m_new
    return accs


def _for_query_tiles(n_lat_rows, n_ctx_rows, tiles, tile_fn):
    tq, tk, unroll = tiles
    tq = min(tq, n_lat_rows)
    lat_blocks = _key_blocks(0, n_lat_rows, tk) + _key_blocks(n_lat_rows, n_ctx_rows, tk)
    ctx_blocks = _key_blocks(n_lat_rows, n_ctx_rows, tk)

    def lat_body(i, carry):
        tile_fn(pl.multiple_of(i * tq, tq), tq, lat_blocks)
        return carry

    lax.fori_loop(0, n_lat_rows // tq, lat_body, 0, unroll=unroll)
    tq_ctx = min(tq, n_ctx_rows)
    for i in range(n_ctx_rows // tq_ctx):
        tile_fn(n_lat_rows + i * tq_ctx, tq_ctx, ctx_blocks)


def _diff_attn_kernel(q_ref, k_ref, v_ref, lam_ref, g_ref, o_ref, *, n_lat_rows, tiles, lam_init):
    dv = 2 * HEAD_DIM
    n_ctx_rows = q_ref.shape[1] - n_lat_rows
    lane = lax.broadcasted_iota(jnp.int32, (1, LANES), 1)
    ones_blk = jnp.broadcast_to((lane == 0).astype(BF16), (tiles[1], LANES))
    lv = lam_ref[0]
    lam = (jnp.exp(jnp.sum(lv[0:1] * lv[1:2], axis=1, keepdims=True))
           - jnp.exp(jnp.sum(lv[2:3] * lv[3:4], axis=1, keepdims=True)) + lam_init)
    gain = g_ref[...] * (1.0 - lam_init)

    def v1_fn(e, start, size):
        return jnp.concatenate([v_ref[0, pl.ds(start, size), :], ones_blk[:size]], axis=1)

    def tile_fn(r0, rows, blocks):
        q = q_ref[0, pl.ds(r0, rows), :]
        zero = jnp.zeros_like(q)
        qms = (jnp.where(lane < HEAD_DIM, q, zero), jnp.where(lane >= HEAD_DIM, q, zero))
        a1, a2 = _attn_pair_tile(qms, k_ref, v1_fn, blocks)
        o = a1[:, :dv] * (1.0 / a1[:, dv:dv + 1]) - lam * (a2[:, :dv] * (1.0 / a2[:, dv:dv + 1]))
        ms = jnp.mean(o * o, axis=1, keepdims=True)
        o_ref[0, pl.ds(r0, rows), :] = (o * lax.rsqrt(ms + LN_EPS) * gain).astype(BF16)

    _for_query_tiles(n_lat_rows, n_ctx_rows, tiles, tile_fn)


def _diff_attn_call(dq, dk, dv, lam_vecs, g, n_lat, lam_init):
    b, t, _ = dq.shape
    seq = pl.BlockSpec((1, t, LANES), lambda bi, h: (bi, 0, h))
    return pl.pallas_call(
        functools.partial(_diff_attn_kernel, n_lat_rows=n_lat * ROW_TILE, tiles=DIFF_TILES,
                          lam_init=lam_init),
        grid=(b, DIFF_HEADS),
        in_specs=[seq, seq, seq,
                  pl.BlockSpec((1, 4, HEAD_DIM), lambda bi, h: (0, 0, 0)),
                  pl.BlockSpec((1, LANES), lambda bi, h: (0, 0))],
        out_specs=seq,
        out_shape=jax.ShapeDtypeStruct((b, t, DIFF_W), BF16),
        compiler_params=_cparams("parallel", "parallel"),
        name="diff_attn",
    )(dq, dk, dv, lam_vecs, g)


def _gqa_attn_kernel(q_ref, k_ref, v_ref, o_ref, *, n_lat_rows, tiles):
    n_ctx_rows = q_ref.shape[1] - n_lat_rows
    lane = lax.broadcasted_iota(jnp.int32, (1, LANES), 1)
    first_half = lane < HEAD_DIM
    den_lane = (HEAD_DIM, 0)

    def v1_fn(e, start, size):
        return v_ref[0, pl.ds(start, size), e * LANES:(e + 1) * LANES]

    def tile_fn(r0, rows, blocks):
        q = q_ref[0, pl.ds(r0, rows), :]
        zero = jnp.zeros_like(q)
        qms = (jnp.where(first_half, q, zero), jnp.where(first_half, zero, q))
        accs = _attn_pair_tile(qms, k_ref, v1_fn, blocks)
        halves = [acc * (1.0 / acc[:, den_lane[e]:den_lane[e] + 1]) for e, acc in enumerate(accs)]
        o_ref[0, pl.ds(r0, rows), :] = jnp.where(first_half, halves[0], halves[1]).astype(BF16)

    _for_query_tiles(n_lat_rows, n_ctx_rows, tiles, tile_fn)


def _gqa_attn_call(gq, gk, gv, n_lat):
    b, t, _ = gq.shape
    pair = pl.BlockSpec((1, t, LANES), lambda bi, p: (bi, 0, p))
    return pl.pallas_call(
        functools.partial(_gqa_attn_kernel, n_lat_rows=n_lat * ROW_TILE, tiles=GQA_TILES),
        grid=(b, GQA_W // LANES),
        in_specs=[pair,
                  pl.BlockSpec((1, t, GQA_KV_W), lambda bi, p: (bi, 0, 0)),
                  pl.BlockSpec((1, t, 2 * LANES), lambda bi, p: (bi, 0, 0))],
        out_specs=pair,
        out_shape=jax.ShapeDtypeStruct((b, t, GQA_W), BF16),
        compiler_params=_cparams("parallel", "arbitrary"),
        name="gqa_attn",
    )(gq, gk, gv)


def _post_kernel(a_ref, bd_ref, cg_ref, x_ref, mod_ref, w_ref, g1_ref, b1_ref,
                 wi_ref, wo_ref, g2_ref, b2_ref, o_ref, *, alpha, d_ff, chunk):
    mod = mod_ref[0, 0]
    y = (_dot(a_ref[0], w_ref[0:MLSTM_W, :])
         + _dot(bd_ref[0], w_ref[MLSTM_W:MLSTM_W + DIFF_W, :])
         + _dot(cg_ref[0], w_ref[MLSTM_W + DIFF_W:, :]))
    x = _layer_norm(alpha * x_ref[0] + mod[2:3] * y, g1_ref[...], b1_ref[...])
    xm = (x * (1.0 + mod[4:5]) + mod[3:4]).astype(BF16)
    acc = jnp.zeros(x.shape, F32)
    for c in range(d_ff // chunk):
        gate = _dot(xm, wi_ref[:, c * chunk:(c + 1) * chunk])
        up = _dot(xm, wi_ref[:, d_ff + c * chunk:d_ff + (c + 1) * chunk])
        act = (gate * jax.nn.sigmoid(gate) * up).astype(BF16)
        acc = acc + _dot(act, wo_ref[c * chunk:(c + 1) * chunk, :])
    o_ref[0] = _layer_norm(alpha * x + mod[5:6] * acc, g2_ref[...], b2_ref[...])


def _post_call(a, bd, cg, xs, mod, w, g1, b1, wi, wo, g2, b2, n_lat, n_rows, alpha):
    b, _, d = xs.shape
    d_ff = wo.shape[0]
    tm = ROW_TILE
    row = lambda width: pl.BlockSpec((1, tm, width), lambda bi, i: (bi, i, 0))
    vec = pl.BlockSpec((1, d), lambda bi, i: (0, 0))
    full = lambda arr: pl.BlockSpec(arr.shape, lambda bi, i: (0, 0))
    return pl.pallas_call(
        functools.partial(_post_kernel, alpha=alpha, d_ff=d_ff, chunk=2 * LANES),
        grid=(b, n_rows // tm),
        in_specs=[row(MLSTM_W), row(DIFF_W), row(GQA_W), row(d),
                  pl.BlockSpec((1, 1, 6, d), lambda bi, i: (bi, jnp.minimum(i // n_lat, 1), 0, 0)),
                  full(w), vec, vec, full(wi), full(wo), vec, vec],
        out_specs=row(d),
        out_shape=jax.ShapeDtypeStruct((b, n_rows, d), F32),
        compiler_params=_cparams("parallel", "parallel"),
        name="out_proj_ffn",
    )(a, bd, cg, xs, mod, w, g1, b1, wi, wo, g2, b2)


def _prep_w_in(w):
    zeros = lambda n: jnp.zeros((w.shape[0], n), w.dtype)
    o_gate = 4 * MLSTM_W
    o_dq = o_gate + 4 * MLSTM_HEADS
    o_gq = o_dq + 3 * DIFF_W
    o_gk = o_gq + GQA_W
    o_gv = o_gk + GQA_KV_W
    gq = w[:, o_gq:o_gk].reshape(-1, 2, 2, HEAD_DIM).transpose(0, 2, 1, 3).reshape(-1, GQA_W)
    gv = w[:, o_gv:o_gv + GQA_KV_W]
    cols = [w[:, :o_gate], w[:, o_dq:o_gq], gq, w[:, o_gk:o_gv],
            gv[:, :HEAD_DIM], zeros(LANES), gv[:, HEAD_DIM:],
            w[:, o_gate:o_dq], zeros(LANES - 4 * MLSTM_HEADS)]
    out = jnp.concatenate(cols, axis=1)
    assert out.shape[1] == IN_COLS_PADDED
    return out.astype(BF16)


def _prep_w_out(w):
    o_c = MLSTM_W + DIFF_W
    gq = w[o_c:].reshape(2, 2, HEAD_DIM, -1).transpose(1, 0, 2, 3).reshape(GQA_W, -1)
    return jnp.concatenate([w[:o_c], gq], axis=0).astype(BF16)


def _rope_tables(s, n_ctx_rows):
    pos = jnp.arange(s, dtype=jnp.int32)
    row = (pos // GRID_W).astype(F32)
    col = (pos % GRID_W).astype(F32)
    n_freq = HEAD_DIM // 4
    inv = ROPE_THETA ** (-jnp.arange(n_freq, dtype=F32) / n_freq)
    ar = row[:, None] * inv
    ac = col[:, None] * inv
    ang = jnp.concatenate([ar, ar, ac, ac], axis=-1)
    ang = jnp.concatenate([ang, ang], axis=-1)
    first = (jnp.arange(LANES) % (2 * ROPE_HALF)) < ROPE_HALF
    cos, sin = jnp.cos(ang), jnp.sin(ang)
    sa = jnp.where(first, -sin, 0.0)
    sb = jnp.where(first, 0.0, sin)
    pad = lambda a, v: jnp.concatenate([a, jnp.full((n_ctx_rows, LANES), v, F32)], axis=0)
    return pad(cos, 1.0), pad(sa, 0.0), pad(sb, 0.0)


def kernel(x, c, ctx, c_ctx, w_ada, b_ada, w_in, mlstm_conv_w, mlstm_conv_b, mlstm_gate_b, mlstm_norm_g, diff_lambda, diff_norm_g, gqa_q_norm_g, gqa_k_norm_g, w_out, ln1_g, ln1_b, w_ffn_in, w_ffn_out, ln2_g, ln2_b):
    b, s, d = x.shape
    n_ctx_rows = ctx.shape[1]
    depth = w_in.shape[0]
    assert s % ROW_TILE == 0 and n_ctx_rows % ROW_TILE == 0 and s % GRID_W == 0
    n_lat = s // ROW_TILE
    t = s + n_ctx_rows
    alpha = (2 * depth) ** 0.25

    ada_rows = -(-(b + 1) // 8) * 8
    cvec = jnp.concatenate([c, c_ctx[None, :], jnp.zeros((ada_rows - b - 1, d), F32)], axis=0)
    mod_all = _ada_call(cvec, w_ada, b_ada)
    mod_lat = mod_all[:, :b].reshape(depth, b, 1, 6, d)
    mod_ctx = jnp.broadcast_to(mod_all[:, b].reshape(depth, 1, 1, 6, d), (depth, b, 1, 6, d))
    mod_all = jnp.concatenate([mod_lat, mod_ctx], axis=2)

    tabs = _rope_tables(s, n_ctx_rows)
    tile2 = lambda v, n: jnp.tile(v, n).reshape(1, -1)
    xs = jnp.concatenate([x, ctx], axis=1)

    for l in range(depth):
        last = l == depth - 1
        lam_init = 0.8 - 0.6 * math.exp(-0.3 * l)
        mod = mod_all[l]
        (mqk, mv, mo, gate, dq, dk, dv, gq, gk, gv) = _inproj_call(
            xs, mod, _prep_w_in(w_in[l]), tabs,
            tile2(gqa_q_norm_g[l], LANES // HEAD_DIM), tile2(gqa_k_norm_g[l], LANES // HEAD_DIM), n_lat)
        cw = jnp.concatenate([mlstm_conv_w[l], jnp.zeros((8 - CONV_W, 2 * MLSTM_W), F32)], axis=0)
        gb = jnp.concatenate([mlstm_gate_b[l], jnp.zeros((LANES - 4 * MLSTM_HEADS,), F32)]).reshape(1, LANES)
        a = _mlstm_call(mqk, mv, mo, gate, cw, mlstm_conv_b[l].reshape(1, -1), gb,
                        mlstm_norm_g[l].reshape(1, -1), n_lat)
        bd = _diff_attn_call(dq, dk, dv, diff_lambda[l][None], diff_norm_g[l].reshape(1, -1), n_lat, lam_init)
        cg = _gqa_attn_call(gq, gk, gv, n_lat)
        n_rows = s if last else t
        xs = _post_call(a, bd, cg, xs, mod, _prep_w_out(w_out[l]),
                        ln1_g[l].reshape(1, -1), ln1_b[l].reshape(1, -1),
                        w_ffn_in[l].astype(BF16), w_ffn_out[l].astype(BF16),
                        ln2_g[l].reshape(1, -1), ln2_b[l].reshape(1, -1), n_lat, n_rows, alpha)
    return xs
```

```python
import functools
import math

import jax
import jax.numpy as jnp
from jax import lax
from jax.experimental import pallas as pl
from jax.experimental.pallas import tpu as pltpu

F32 = jnp.float32
BF16 = jnp.bfloat16

HEAD_DIM = 64
MLSTM_HEADS = 4
DIFF_HEADS = 4
GQA_HEADS = 4
GQA_KV_HEADS = 2
GRID_W = 64
CONV_W = 5
ROPE_THETA = 10000.0
LN_EPS = 1e-5

MLSTM_W = MLSTM_HEADS * HEAD_DIM
DIFF_W = DIFF_HEADS * 2 * HEAD_DIM
GQA_W = GQA_HEADS * HEAD_DIM
GQA_KV_W = GQA_KV_HEADS * HEAD_DIM

LANES = 128
BF16_SUBLANES = 16
VMEM_LIMIT_BYTES = 56 * 1024 * 1024

ROW_TILE = 256
LINEAR_TILE = 512
ROPE_HALF = HEAD_DIM // 4
DIFF_TILES = (512, 1024, 2)
GQA_TILES = (256, 4096, 2)
ATTN_Q_SCALE = HEAD_DIM ** -0.5 * math.log2(math.e)

C_MQK, C_MV, C_MO = 0, 512, 768
C_DQ, C_DK, C_DV = 1024, 1536, 2048
C_GQ, C_GK, C_GV, C_GATE = 2560, 2816, 2944, 3200
IN_COLS_PADDED = 3328


def _cparams(*sem):
    return pltpu.CompilerParams(dimension_semantics=sem, vmem_limit_bytes=VMEM_LIMIT_BYTES)


def _resident(shape):
    return pl.BlockSpec(shape, lambda *_: (0,) * len(shape), pipeline_mode=pl.Buffered(1))


def _linear_tiling(n_lat):
    n_lat_rows = n_lat * ROW_TILE
    tm = min(LINEAR_TILE, n_lat_rows)
    assert n_lat_rows % tm == 0
    return tm, n_lat_rows // tm


def _dot(a, b):
    return jnp.dot(a, b, preferred_element_type=F32)


def _dot_nt(a, b):
    return lax.dot_general(a, b, (((1,), (1,)), ((), ())), preferred_element_type=F32)


def _split_bf16(a):
    hi = a.astype(BF16)
    lo = (a - hi.astype(F32)).astype(BF16)
    return hi, lo


def _dot_split_lhs(a, m):
    hi, lo = _split_bf16(a)
    return _dot(hi, m) + _dot(lo, m)


def _dot_split_rhs(m, a):
    hi, lo = _split_bf16(a)
    return _dot(m, hi) + _dot(m, lo)


def _head_mean_matrix(width):
    r = lax.broadcasted_iota(jnp.int32, (width, width), 0) // HEAD_DIM
    c = lax.broadcasted_iota(jnp.int32, (width, width), 1) // HEAD_DIM
    return jnp.where(r == c, 1.0 / HEAD_DIM, 0.0).astype(BF16)


def _layer_norm(z, g, b):
    mu = jnp.mean(z, axis=-1, keepdims=True)
    zc = z - mu
    var = jnp.mean(zc * zc, axis=-1, keepdims=True)
    return zc * lax.rsqrt(var + LN_EPS) * g + b


def _ada_kernel(c_ref, w_ref, b_ref, o_ref):
    c = c_ref[...]
    o_ref[0] = _dot(c * jax.nn.sigmoid(c), w_ref[0]) + b_ref[0]


def _ada_call(cvec, w_ada, b_ada):
    depth, d, six_d = w_ada.shape
    rows = cvec.shape[0]
    bn = 1024
    return pl.pallas_call(
        _ada_kernel,
        grid=(depth, six_d // bn),
        in_specs=[pl.BlockSpec((rows, d), lambda l, j: (0, 0)),
                  pl.BlockSpec((1, d, bn), lambda l, j: (l, 0, j)),
                  pl.BlockSpec((1, 1, bn), lambda l, j: (l, 0, j))],
        out_specs=pl.BlockSpec((1, rows, bn), lambda l, j: (l, 0, j)),
        out_shape=jax.ShapeDtypeStruct((depth, rows, six_d), F32),
        compiler_params=_cparams("parallel", "parallel"),
        name="adaln_mod",
    )(cvec, w_ada, b_ada.reshape(depth, 1, six_d))


def _inproj_kernel(x_ref, mod_ref, w_ref, cos_ref, sa_ref, sb_ref, qg_ref, kg_ref,
                   mqk_ref, mv_ref, mo_ref, gate_ref, dq_ref, dk_ref, dv_ref,
                   gq_ref, gk_ref, gv_ref):
    x = x_ref[0]
    mod = mod_ref[0, 0]
    xm = (x * (1.0 + mod[1:2]) + mod[0:1]).astype(BF16)
    cos, sa, sb = cos_ref[...], sa_ref[...], sb_ref[...]
    scale = ATTN_Q_SCALE

    def proj(lo, n):
        return _dot(xm, w_ref[:, lo:lo + n])

    def rope(a):
        return (a * cos + pltpu.roll(a, LANES - ROPE_HALF, 1) * sa
                + pltpu.roll(a, ROPE_HALF, 1) * sb)

    mqk_ref[0] = proj(C_MQK, 2 * MLSTM_W).astype(BF16)
    mv_ref[0] = proj(C_MV, MLSTM_W).astype(BF16)
    mo_ref[0] = proj(C_MO, MLSTM_W).astype(BF16)
    gate_ref[0] = proj(C_GATE, LANES)

    acc = proj(C_DQ, DIFF_W)
    for j in range(DIFF_W // LANES):
        dq_ref[0, :, j * LANES:(j + 1) * LANES] = (
            rope(acc[:, j * LANES:(j + 1) * LANES]) * scale).astype(BF16)
    acc = proj(C_DK, DIFF_W)
    for j in range(DIFF_W // LANES):
        dk_ref[0, :, j * LANES:(j + 1) * LANES] = rope(acc[:, j * LANES:(j + 1) * LANES]).astype(BF16)
    dv_ref[0] = proj(C_DV, DIFF_W).astype(BF16)

    head_mean = _head_mean_matrix(LANES)

    def rms(a, g):
        ms = _dot_split_lhs(a * a, head_mean)
        return a * lax.rsqrt(ms + LN_EPS) * g

    acc = proj(C_GQ, GQA_W)
    qg = qg_ref[...]
    for j in range(GQA_W // LANES):
        gq_ref[0, :, j * LANES:(j + 1) * LANES] = (
            rope(rms(acc[:, j * LANES:(j + 1) * LANES], qg)) * scale).astype(BF16)
    gk_ref[0] = rope(rms(proj(C_GK, GQA_KV_W), kg_ref[...])).astype(BF16)

    lane = lax.broadcasted_iota(jnp.int32, (1, 2 * LANES), 1)
    ones_col = jnp.logical_or(lane == HEAD_DIM, lane == LANES).astype(F32)
    gv_ref[0] = (proj(C_GV, 2 * LANES) + ones_col).astype(BF16)


def _inproj_call(xs, mod, w, tabs, qg, kg, n_lat):
    b, t, d = xs.shape
    tm, n_lat_tiles = _linear_tiling(n_lat)
    row = lambda width: pl.BlockSpec((1, tm, width), lambda bi, i: (bi, i, 0))
    tab = pl.BlockSpec((tm, LANES), lambda bi, i: (i, 0))
    vec = pl.BlockSpec((1, LANES), lambda bi, i: (0, 0))
    widths = (2 * MLSTM_W, MLSTM_W, MLSTM_W, LANES, DIFF_W, DIFF_W, DIFF_W, GQA_W, GQA_KV_W, 2 * LANES)
    dtypes = (BF16, BF16, BF16, F32, BF16, BF16, BF16, BF16, BF16, BF16)
    return pl.pallas_call(
        _inproj_kernel,
        grid=(b, pl.cdiv(t, tm)),
        in_specs=[row(d),
                  pl.BlockSpec((1, 1, 6, d), lambda bi, i: (bi, jnp.minimum(i // n_lat_tiles, 1), 0, 0)),
                  _resident((d, IN_COLS_PADDED)),
                  tab, tab, tab, vec, vec],
        out_specs=[row(wd) for wd in widths],
        out_shape=[jax.ShapeDtypeStruct((b, t, wd), dt) for wd, dt in zip(widths, dtypes)],
        compiler_params=_cparams("parallel", "parallel"),
        name="in_proj",
    )(xs, mod, w, *tabs, qg, kg)


def _mlstm_kernel(mqk_ref, mv_ref, mo_ref, gate_ref, cw_ref, cb_ref, gb_ref, ng_ref, out_ref,
                  qk_s, h_s, *, n_lat, n_tot):
    lc = ROW_TILE
    t_rows = n_tot * lc
    halo = BF16_SUBLANES

    cw = cw_ref[...]
    cb = cb_ref[...]
    lane_qk = lax.broadcasted_iota(jnp.int32, (1, 2 * MLSTM_W), 1)
    qscale = jnp.where(lane_qk < MLSTM_W, HEAD_DIM ** -0.5, 1.0)

    def conv_body(c, carry):
        r0 = pl.multiple_of(c * lc, lc)
        main = mqk_ref[0, pl.ds(r0, lc), :].astype(F32)
        ts = pl.multiple_of(jnp.maximum(r0 - halo, 0), halo)
        bs = pl.multiple_of(jnp.minimum(r0 + lc, t_rows - halo), halo)
        top_ok = jnp.logical_and(c != 0, c != n_lat).astype(F32)
        bot_ok = jnp.logical_and(c != n_lat - 1, c != n_tot - 1).astype(F32)
        top = mqk_ref[0, pl.ds(ts, halo), :].astype(F32) * top_ok
        bot = mqk_ref[0, pl.ds(bs, halo), :].astype(F32) * bot_ok
        win = jnp.concatenate([top, main, bot], axis=0)
        n = lc + 2 * halo
        y = win * cw[CONV_W // 2:CONV_W // 2 + 1]
        for j in range(CONV_W):
            if j != CONV_W // 2:
                y = y + pltpu.roll(win, (CONV_W // 2 - j) % n, 0) * cw[j:j + 1]
        y = y[halo:halo + lc] + cb
        qk_s[pl.ds(r0, lc), :] = (y * jax.nn.sigmoid(y) * qscale).astype(BF16)
        return carry

    lax.fori_loop(0, n_tot, conv_body, 0)

    row_i = lax.broadcasted_iota(jnp.int32, (lc, lc), 0)
    col_i = lax.broadcasted_iota(jnp.int32, (lc, lc), 1)
    masks_t = (row_i <= col_i, row_i >= col_i)
    tri_row = (masks_t[0].astype(BF16), masks_t[1].astype(BF16))
    tri_col = (tri_row[1], tri_row[0])
    sub = lax.broadcasted_iota(jnp.int32, (LANES, lc), 0)
    sels = (sub < HEAD_DIM, sub >= HEAD_DIM)
    one_rows = ((sub == HEAD_DIM).astype(BF16), (sub == 0).astype(BF16))
    den_row = (HEAD_DIM, 0)
    gb = gb_ref[...]
    h_s[...] = jnp.zeros_like(h_s)
    n_streams = 2 * MLSTM_HEADS

    def step(j, carry):
        cns, ms = carry
        new_cns, new_ms = list(cns), list(ms)
        for d in range(2):
            c = (j + n_lat) % n_tot if d == 0 else n_tot - 1 - j
            r0 = pl.multiple_of(c * lc, lc)
            gates = gate_ref[0, pl.ds(r0, lc), :] + gb
            gates_t = gates.T
            cum = _dot_split_rhs(tri_col[d], jax.nn.log_sigmoid(gates))
            cum_t = _dot_split_lhs(jax.nn.log_sigmoid(gates_t), tri_row[d])
            qk = qk_s[pl.ds(r0, lc), :]
            vv = mv_ref[0, pl.ds(r0, lc), :]
            last = lc - 1 if d == 0 else 0
            for p in range(MLSTM_HEADS // 2):
                q_t = qk[:, p * LANES:(p + 1) * LANES].T
                k128 = qk[:, MLSTM_W + p * LANES:MLSTM_W + (p + 1) * LANES]
                v_t = vv[:, p * LANES:(p + 1) * LANES].T
                halves = []
                for e in range(2):
                    hh = 2 * p + e
                    sidx = d * MLSTM_HEADS + hh
                    ci = d * MLSTM_HEADS + hh
                    cf = 2 * MLSTM_HEADS + d * MLSTM_HEADS + hh
                    qm_t = jnp.where(sels[e], q_t, jnp.zeros_like(q_t))
                    v1_t = jnp.where(sels[e], v_t, one_rows[e])
                    i_row = gates_t[ci:ci + 1, :]
                    a_row = cum_t[cf:cf + 1, :]
                    r_col = gates[:, ci:ci + 1] - cum[:, cf:cf + 1]
                    m = ms[sidx]
                    cn_t = cns[sidx]
                    dm = jnp.where(masks_t[d], a_row + r_col, -jnp.inf)
                    inter = a_row + m
                    m_t = jnp.maximum(jnp.max(dm, axis=0, keepdims=True), inter)
                    w_prev = jnp.exp(inter - m_t)
                    sw = (_dot(k128, qm_t) * jnp.exp(dm - m_t)).astype(BF16)
                    nd = _dot(v1_t, sw) + w_prev * _dot(cn_t.astype(BF16), qm_t)
                    den = nd[den_row[e]:den_row[e] + 1, :]
                    halves.append(nd * (1.0 / jnp.maximum(jnp.abs(den), jnp.exp(-m_t))))
                    b_last = cum_t[cf:cf + 1, last:last + 1]
                    logu = b_last - a_row + i_row
                    m_new = jnp.maximum(b_last + m, jnp.max(logu, axis=1, keepdims=True))
                    vu = (v1_t.astype(F32) * jnp.exp(logu - m_new)).astype(BF16)
                    new_cns[sidx] = jnp.exp(b_last + m - m_new) * cn_t + _dot(vu, k128)
                    new_ms[sidx] = m_new
                h_pair = jnp.where(sels[0], halves[0], halves[1]).T
                h_s[pl.ds(r0, lc), p * LANES:(p + 1) * LANES] += h_pair
        return tuple(new_cns), tuple(new_ms)

    init = (tuple(jnp.zeros((LANES, LANES), F32) for _ in range(n_streams)),
            tuple(jnp.zeros((1, 1), F32) for _ in range(n_streams)))
    lax.fori_loop(0, n_tot, step, init)

    head_mean = _head_mean_matrix(MLSTM_W)
    ng = ng_ref[...]

    def out_body(c, carry):
        r0 = pl.multiple_of(c * lc, lc)
        hb = h_s[pl.ds(r0, lc), :]
        hc = hb - _dot_split_lhs(hb, head_mean)
        var = _dot_split_lhs(hc * hc, head_mean)
        o = mo_ref[0, pl.ds(r0, lc), :].astype(F32)
        out_ref[0, pl.ds(r0, lc), :] = (hc * lax.rsqrt(var + LN_EPS) * ng * jax.nn.sigmoid(o)).astype(BF16)
        return carry

    lax.fori_loop(0, n_tot, out_body, 0)


def _mlstm_call(mqk, mv, mo, gate, cw, cb, gb, ng, n_lat):
    b, t, _ = mqk.shape
    n_tot = t // ROW_TILE
    seq = lambda width: pl.BlockSpec((1, t, width), lambda bi: (bi, 0, 0))
    full = lambda a: pl.BlockSpec(a.shape, lambda bi: (0, 0))
    return pl.pallas_call(
        functools.partial(_mlstm_kernel, n_lat=n_lat, n_tot=n_tot),
        grid=(b,),
        in_specs=[seq(2 * MLSTM_W), seq(MLSTM_W), seq(MLSTM_W), seq(LANES),
                  full(cw), full(cb), full(gb), full(ng)],
        out_specs=seq(MLSTM_W),
        out_shape=jax.ShapeDtypeStruct((b, t, MLSTM_W), BF16),
        scratch_shapes=[pltpu.VMEM((t, 2 * MLSTM_W), BF16), pltpu.VMEM((t, MLSTM_W), F32)],
        compiler_params=_cparams("parallel"),
        name="mlstm",
    )(mqk, mv, mo, gate, cw, cb, gb, ng)


def _key_blocks(first, n_rows, tk):
    blocks, r = [], first
    while r < first + n_rows:
        size = min(tk, first + n_rows - r)
        blocks.append((r, size))
        r += size
    return blocks


def _attn_pair_tile(qms, k_ref, v1_fn, blocks):
    ms, accs = [None, None], [None, None]
    for start, size in blocks:
        k = k_ref[0, pl.ds(start, size), :]
        for e in range(2):
            s = _dot_nt(qms[e], k)
            row_max = jnp.max(s, axis=1, keepdims=True)
            m_new = row_max if ms[e] is None else jnp.maximum(ms[e], row_max)
            pv = _dot(jnp.exp2(s - m_new).astype(BF16), v1_fn(e, start, size))
            accs[e] = pv if accs[e] is None else jnp.exp2(ms[e] - m_new) * accs[e] + pv
            ms[e] = m_new
    return accs


def _for_query_tiles(n_lat_rows, n_ctx_rows, tiles, tile_fn):
    tq, tk, unroll = tiles
    tq = min(tq, n_lat_rows)
    lat_blocks = _key_blocks(0, n_lat_rows, tk) + _key_blocks(n_lat_rows, n_ctx_rows, tk)
    ctx_blocks = _key_blocks(n_lat_rows, n_ctx_rows, tk)

    def lat_body(i, carry):
        tile_fn(pl.multiple_of(i * tq, tq), tq, lat_blocks)
        return carry

    lax.fori_loop(0, n_lat_rows // tq, lat_body, 0, unroll=unroll)
    tq_ctx = min(tq, n_ctx_rows)
    for i in range(n_ctx_rows // tq_ctx):
        tile_fn(n_lat_rows + i * tq_ctx, tq_ctx, ctx_blocks)


def _diff_attn_kernel(q_ref, k_ref, v_ref, lam_ref, g_ref, o_ref, *, n_lat_rows, tiles, lam_init):
    dv = 2 * HEAD_DIM
    n_ctx_rows = q_ref.shape[1] - n_lat_rows
    lane = lax.broadcasted_iota(jnp.int32, (1, LANES), 1)
    ones_blk = jnp.broadcast_to((lane == 0).astype(BF16), (tiles[1], LANES))
    lv = lam_ref[0]
    lam = (jnp.exp(jnp.sum(lv[0:1] * lv[1:2], axis=1, keepdims=True))
           - jnp.exp(jnp.sum(lv[2:3] * lv[3:4], axis=1, keepdims=True)) + lam_init)
    gain = g_ref[...] * (1.0 - lam_init)

    def v1_fn(e, start, size):
        return jnp.concatenate([v_ref[0, pl.ds(start, size), :], ones_blk[:size]], axis=1)

    def tile_fn(r0, rows, blocks):
        q = q_ref[0, pl.ds(r0, rows), :]
        zero = jnp.zeros_like(q)
        qms = (jnp.where(lane < HEAD_DIM, q, zero), jnp.where(lane >= HEAD_DIM, q, zero))
        a1, a2 = _attn_pair_tile(qms, k_ref, v1_fn, blocks)
        o = a1[:, :dv] * (1.0 / a1[:, dv:dv + 1]) - lam * (a2[:, :dv] * (1.0 / a2[:, dv:dv + 1]))
        ms = jnp.mean(o * o, axis=1, keepdims=True)
        o_ref[0, pl.ds(r0, rows), :] = (o * lax.rsqrt(ms + LN_EPS) * gain).astype(BF16)

    _for_query_tiles(n_lat_rows, n_ctx_rows, tiles, tile_fn)


def _diff_attn_call(dq, dk, dv, lam_vecs, g, n_lat, lam_init):
    b, t, _ = dq.shape
    seq = pl.BlockSpec((1, t, LANES), lambda bi, h: (bi, 0, h))
    return pl.pallas_call(
        functools.partial(_diff_attn_kernel, n_lat_rows=n_lat * ROW_TILE, tiles=DIFF_TILES,
                          lam_init=lam_init),
        grid=(b, DIFF_HEADS),
        in_specs=[seq, seq, seq,
                  pl.BlockSpec((1, 4, HEAD_DIM), lambda bi, h: (0, 0, 0)),
                  pl.BlockSpec((1, LANES), lambda bi, h: (0, 0))],
        out_specs=seq,
        out_shape=jax.ShapeDtypeStruct((b, t, DIFF_W), BF16),
        compiler_params=_cparams("parallel", "parallel"),
        name="diff_attn",
    )(dq, dk, dv, lam_vecs, g)


def _gqa_attn_kernel(q_ref, k_ref, v_ref, o_ref, *, n_lat_rows, tiles):
    n_ctx_rows = q_ref.shape[1] - n_lat_rows
    lane = lax.broadcasted_iota(jnp.int32, (1, LANES), 1)
    first_half = lane < HEAD_DIM
    den_lane = (HEAD_DIM, 0)

    def v1_fn(e, start, size):
        return v_ref[0, pl.ds(start, size), e * LANES:(e + 1) * LANES]

    def tile_fn(r0, rows, blocks):
        q = q_ref[0, pl.ds(r0, rows), :]
        zero = jnp.zeros_like(q)
        qms = (jnp.where(first_half, q, zero), jnp.where(first_half, zero, q))
        accs = _attn_pair_tile(qms, k_ref, v1_fn, blocks)
        halves = [acc * (1.0 / acc[:, den_lane[e]:den_lane[e] + 1]) for e, acc in enumerate(accs)]
        o_ref[0, pl.ds(r0, rows), :] = jnp.where(first_half, halves[0], halves[1]).astype(BF16)

    _for_query_tiles(n_lat_rows, n_ctx_rows, tiles, tile_fn)


def _gqa_attn_call(gq, gk, gv, n_lat):
    b, t, _ = gq.shape
    pair = pl.BlockSpec((1, t, LANES), lambda bi, p: (bi, 0, p))
    return pl.pallas_call(
        functools.partial(_gqa_attn_kernel, n_lat_rows=n_lat * ROW_TILE, tiles=GQA_TILES),
        grid=(b, GQA_W // LANES),
        in_specs=[pair,
                  pl.BlockSpec((1, t, GQA_KV_W), lambda bi, p: (bi, 0, 0)),
                  pl.BlockSpec((1, t, 2 * LANES), lambda bi, p: (bi, 0, 0))],
        out_specs=pair,
        out_shape=jax.ShapeDtypeStruct((b, t, GQA_W), BF16),
        compiler_params=_cparams("parallel", "arbitrary"),
        name="gqa_attn",
    )(gq, gk, gv)


def _post_kernel(a_ref, bd_ref, cg_ref, x_ref, mod_ref, w_ref, g1_ref, b1_ref,
                 wi_ref, wo_ref, g2_ref, b2_ref, o_ref, *, alpha, d_ff, chunk):
    mod = mod_ref[0, 0]
    y = (_dot(a_ref[0], w_ref[0:MLSTM_W, :])
         + _dot(bd_ref[0], w_ref[MLSTM_W:MLSTM_W + DIFF_W, :])
         + _dot(cg_ref[0], w_ref[MLSTM_W + DIFF_W:, :]))
    x = _layer_norm(alpha * x_ref[0] + mod[2:3] * y, g1_ref[...], b1_ref[...])
    xm = (x * (1.0 + mod[4:5]) + mod[3:4]).astype(BF16)
    acc = jnp.zeros(x.shape, F32)
    for c in range(d_ff // chunk):
        gate = _dot(xm, wi_ref[:, c * chunk:(c + 1) * chunk])
        up = _dot(xm, wi_ref[:, d_ff + c * chunk:d_ff + (c + 1) * chunk])
        act = (gate * jax.nn.sigmoid(gate) * up).astype(BF16)
        acc = acc + _dot(act, wo_ref[c * chunk:(c + 1) * chunk, :])
    o_ref[0] = _layer_norm(alpha * x + mod[5:6] * acc, g2_ref[...], b2_ref[...])


def _post_call(a, bd, cg, xs, mod, w, g1, b1, wi, wo, g2, b2, n_lat, n_rows, alpha):
    b, _, d = xs.shape
    d_ff = wo.shape[0]
    tm, n_lat_tiles = _linear_tiling(n_lat)
    row = lambda width: pl.BlockSpec((1, tm, width), lambda bi, i: (bi, i, 0))
    vec = pl.BlockSpec((1, d), lambda bi, i: (0, 0))
    full = lambda arr: _resident(arr.shape)
    return pl.pallas_call(
        functools.partial(_post_kernel, alpha=alpha, d_ff=d_ff, chunk=2 * LANES),
        grid=(b, pl.cdiv(n_rows, tm)),
        in_specs=[row(MLSTM_W), row(DIFF_W), row(GQA_W), row(d),
                  pl.BlockSpec((1, 1, 6, d), lambda bi, i: (bi, jnp.minimum(i // n_lat_tiles, 1), 0, 0)),
                  full(w), vec, vec, full(wi), full(wo), vec, vec],
        out_specs=row(d),
        out_shape=jax.ShapeDtypeStruct((b, n_rows, d), F32),
        compiler_params=_cparams("parallel", "parallel"),
        name="out_proj_ffn",
    )(a, bd, cg, xs, mod, w, g1, b1, wi, wo, g2, b2)


def _prep_w_in(w):
    zeros = lambda n: jnp.zeros((w.shape[0], n), w.dtype)
    o_gate = 4 * MLSTM_W
    o_dq = o_gate + 4 * MLSTM_HEADS
    o_gq = o_dq + 3 * DIFF_W
    o_gk = o_gq + GQA_W
    o_gv = o_gk + GQA_KV_W
    gq = w[:, o_gq:o_gk].reshape(-1, 2, 2, HEAD_DIM).transpose(0, 2, 1, 3).reshape(-1, GQA_W)
    gv = w[:, o_gv:o_gv + GQA_KV_W]
    cols = [w[:, :o_gate], w[:, o_dq:o_gq], gq, w[:, o_gk:o_gv],
            gv[:, :HEAD_DIM], zeros(LANES), gv[:, HEAD_DIM:],
            w[:, o_gate:o_dq], zeros(LANES - 4 * MLSTM_HEADS)]
    out = jnp.concatenate(cols, axis=1)
    assert out.shape[1] == IN_COLS_PADDED
    return out.astype(BF16)


def _prep_w_out(w):
    o_c = MLSTM_W + DIFF_W
    gq = w[o_c:].reshape(2, 2, HEAD_DIM, -1).transpose(1, 0, 2, 3).reshape(GQA_W, -1)
    return jnp.concatenate([w[:o_c], gq], axis=0).astype(BF16)


def _rope_tables(s, n_ctx_rows):
    pos = jnp.arange(s, dtype=jnp.int32)
    row = (pos // GRID_W).astype(F32)
    col = (pos % GRID_W).astype(F32)
    n_freq = HEAD_DIM // 4
    inv = ROPE_THETA ** (-jnp.arange(n_freq, dtype=F32) / n_freq)
    ar = row[:, None] * inv
    ac = col[:, None] * inv
    ang = jnp.concatenate([ar, ar, ac, ac], axis=-1)
    ang = jnp.concatenate([ang, ang], axis=-1)
    first = (jnp.arange(LANES) % (2 * ROPE_HALF)) < ROPE_HALF
    cos, sin = jnp.cos(ang), jnp.sin(ang)
    sa = jnp.where(first, -sin, 0.0)
    sb = jnp.where(first, 0.0, sin)
    pad = lambda a, v: jnp.concatenate([a, jnp.full((n_ctx_rows, LANES), v, F32)], axis=0)
    return pad(cos, 1.0), pad(sa, 0.0), pad(sb, 0.0)


def kernel(x, c, ctx, c_ctx, w_ada, b_ada, w_in, mlstm_conv_w, mlstm_conv_b, mlstm_gate_b, mlstm_norm_g, diff_lambda, diff_norm_g, gqa_q_norm_g, gqa_k_norm_g, w_out, ln1_g, ln1_b, w_ffn_in, w_ffn_out, ln2_g, ln2_b):
    b, s, d = x.shape
    n_ctx_rows = ctx.shape[1]
    depth = w_in.shape[0]
    assert s % ROW_TILE == 0 and n_ctx_rows % ROW_TILE == 0 and s % GRID_W == 0
    n_lat = s // ROW_TILE
    t = s + n_ctx_rows
    alpha = (2 * depth) ** 0.25

    ada_rows = -(-(b + 1) // 8) * 8
    cvec = jnp.concatenate([c, c_ctx[None, :], jnp.zeros((ada_rows - b - 1, d), F32)], axis=0)
    mod_all = _ada_call(cvec, w_ada, b_ada)
    mod_lat = mod_all[:, :b].reshape(depth, b, 1, 6, d)
    mod_ctx = jnp.broadcast_to(mod_all[:, b].reshape(depth, 1, 1, 6, d), (depth, b, 1, 6, d))
    mod_all = jnp.concatenate([mod_lat, mod_ctx], axis=2)

    tabs = _rope_tables(s, n_ctx_rows)
    tile2 = lambda v, n: jnp.tile(v, n).reshape(1, -1)
    xs = jnp.concatenate([x, ctx], axis=1)

    for l in range(depth):
        last = l == depth - 1
        lam_init = 0.8 - 0.6 * math.exp(-0.3 * l)
        mod = mod_all[l]
        (mqk, mv, mo, gate, dq, dk, dv, gq, gk, gv) = _inproj_call(
            xs, mod, _prep_w_in(w_in[l]), tabs,
            tile2(gqa_q_norm_g[l], LANES // HEAD_DIM), tile2(gqa_k_norm_g[l], LANES // HEAD_DIM), n_lat)
        cw = jnp.concatenate([mlstm_conv_w[l], jnp.zeros((8 - CONV_W, 2 * MLSTM_W), F32)], axis=0)
        gb = jnp.concatenate([mlstm_gate_b[l], jnp.zeros((LANES - 4 * MLSTM_HEADS,), F32)]).reshape(1, LANES)
        a = _mlstm_call(mqk, mv, mo, gate, cw, mlstm_conv_b[l].reshape(1, -1), gb,
                        mlstm_norm_g[l].reshape(1, -1), n_lat)
        bd = _diff_attn_call(dq, dk, dv, diff_lambda[l][None], diff_norm_g[l].reshape(1, -1), n_lat, lam_init)
        cg = _gqa_attn_call(gq, gk, gv, n_lat)
        n_rows = s if last else t
        xs = _post_call(a, bd, cg, xs, mod, _prep_w_out(w_out[l]),
                        ln1_g[l].reshape(1, -1), ln1_b[l].reshape(1, -1),
                        w_ffn_in[l].astype(BF16), w_ffn_out[l].astype(BF16),
                        ln2_g[l].reshape(1, -1), ln2_b[l].reshape(1, -1), n_lat, n_rows, alpha)
    return xs
```

```python
import functools
import math

import jax
import jax.numpy as jnp
from jax import lax
from jax.experimental import pallas as pl
from jax.experimental.pallas import tpu as pltpu

F32 = jnp.float32
BF16 = jnp.bfloat16

HEAD_DIM = 64
MLSTM_HEADS = 4
DIFF_HEADS = 4
GQA_HEADS = 4
GQA_KV_HEADS = 2
GRID_W = 64
CONV_W = 5
ROPE_THETA = 10000.0
LN_EPS = 1e-5

MLSTM_W = MLSTM_HEADS * HEAD_DIM
DIFF_W = DIFF_HEADS * 2 * HEAD_DIM
GQA_W = GQA_HEADS * HEAD_DIM
GQA_KV_W = GQA_KV_HEADS * HEAD_DIM

LANES = 128
BF16_SUBLANES = 16
VMEM_LIMIT_BYTES = 56 * 1024 * 1024

ROW_TILE = 256
LINEAR_TILE = 512
ROPE_HALF = HEAD_DIM // 4
DIFF_TILES = (512, 1024, 2)
GQA_TILES = (256, 4096, 2)
ATTN_Q_SCALE = HEAD_DIM ** -0.5 * math.log2(math.e)

C_MQK, C_MV, C_MO = 0, 512, 768
C_DQ, C_DK, C_DV = 1024, 1536, 2048
C_GQ, C_GK, C_GV, C_GATE = 2560, 2816, 2944, 3200
IN_COLS_PADDED = 3328


def _cparams(*sem):
    return pltpu.CompilerParams(dimension_semantics=sem, vmem_limit_bytes=VMEM_LIMIT_BYTES)


def _resident(shape):
    return pl.BlockSpec(shape, lambda *_: (0,) * len(shape), pipeline_mode=pl.Buffered(1))


def _linear_tiling(n_lat):
    n_lat_rows = n_lat * ROW_TILE
    tm = min(LINEAR_TILE, n_lat_rows)
    assert n_lat_rows % tm == 0
    return tm, n_lat_rows // tm


def _dot(a, b):
    return jnp.dot(a, b, preferred_element_type=F32)


def _dot_nt(a, b):
    return lax.dot_general(a, b, (((1,), (1,)), ((), ())), preferred_element_type=F32)


def _split_bf16(a):
    hi = a.astype(BF16)
    lo = (a - hi.astype(F32)).astype(BF16)
    return hi, lo


def _dot_split_lhs(a, m):
    hi, lo = _split_bf16(a)
    return _dot(hi, m) + _dot(lo, m)


def _dot_split_rhs(m, a):
    hi, lo = _split_bf16(a)
    return _dot(m, hi) + _dot(m, lo)


def _head_mean_matrix(width):
    r = lax.broadcasted_iota(jnp.int32, (width, width), 0) // HEAD_DIM
    c = lax.broadcasted_iota(jnp.int32, (width, width), 1) // HEAD_DIM
    return jnp.where(r == c, 1.0 / HEAD_DIM, 0.0).astype(BF16)


def _layer_norm(z, g, b):
    mu = jnp.mean(z, axis=-1, keepdims=True)
    zc = z - mu
    var = jnp.mean(zc * zc, axis=-1, keepdims=True)
    return zc * lax.rsqrt(var + LN_EPS) * g + b


def _ada_kernel(c_ref, w_ref, b_ref, o_ref):
    c = c_ref[...]
    o_ref[0] = _dot(c * jax.nn.sigmoid(c), w_ref[0]) + b_ref[0]


def _ada_call(cvec, w_ada, b_ada):
    depth, d, six_d = w_ada.shape
    rows = cvec.shape[0]
    bn = 1024
    return pl.pallas_call(
        _ada_kernel,
        grid=(depth, six_d // bn),
        in_specs=[pl.BlockSpec((rows, d), lambda l, j: (0, 0)),
                  pl.BlockSpec((1, d, bn), lambda l, j: (l, 0, j)),
                  pl.BlockSpec((1, 1, bn), lambda l, j: (l, 0, j))],
        out_specs=pl.BlockSpec((1, rows, bn), lambda l, j: (l, 0, j)),
        out_shape=jax.ShapeDtypeStruct((depth, rows, six_d), F32),
        compiler_params=_cparams("parallel", "parallel"),
        name="adaln_mod",
    )(cvec, w_ada, b_ada.reshape(depth, 1, six_d))


def _inproj_kernel(x_ref, mod_ref, w_ref, cos_ref, sa_ref, sb_ref, qg_ref, kg_ref,
                   mqk_ref, mv_ref, mo_ref, gate_ref, dq_ref, dk_ref, dv_ref,
                   gq_ref, gk_ref, gv_ref):
    x = x_ref[0]
    mod = mod_ref[0, 0]
    xm = (x * (1.0 + mod[1:2]) + mod[0:1]).astype(BF16)
    cos, sa, sb = cos_ref[...], sa_ref[...], sb_ref[...]
    scale = ATTN_Q_SCALE

    def proj(lo, n):
        return _dot(xm, w_ref[:, lo:lo + n])

    def rope(a):
        return (a * cos + pltpu.roll(a, LANES - ROPE_HALF, 1) * sa
                + pltpu.roll(a, ROPE_HALF, 1) * sb)

    mqk_ref[0] = proj(C_MQK, 2 * MLSTM_W).astype(BF16)
    mv_ref[0] = proj(C_MV, MLSTM_W).astype(BF16)
    mo_ref[0] = proj(C_MO, MLSTM_W).astype(BF16)
    gate_ref[0] = proj(C_GATE, LANES)

    acc = proj(C_DQ, DIFF_W)
    for j in range(DIFF_W // LANES):
        dq_ref[0, :, j * LANES:(j + 1) * LANES] = (
            rope(acc[:, j * LANES:(j + 1) * LANES]) * scale).astype(BF16)
    acc = proj(C_DK, DIFF_W)
    for j in range(DIFF_W // LANES):
        dk_ref[0, :, j * LANES:(j + 1) * LANES] = rope(acc[:, j * LANES:(j + 1) * LANES]).astype(BF16)
    dv_ref[0] = proj(C_DV, DIFF_W).astype(BF16)

    head_mean = _head_mean_matrix(LANES)

    def rms(a, g):
        ms = _dot_split_lhs(a * a, head_mean)
        return a * lax.rsqrt(ms + LN_EPS) * g

    acc = proj(C_GQ, GQA_W)
    qg = qg_ref[...]
    for j in range(GQA_W // LANES):
        gq_ref[0, :, j * LANES:(j + 1) * LANES] = (
            rope(rms(acc[:, j * LANES:(j + 1) * LANES], qg)) * scale).astype(BF16)
    gk_ref[0] = rope(rms(proj(C_GK, GQA_KV_W), kg_ref[...])).astype(BF16)

    lane = lax.broadcasted_iota(jnp.int32, (1, 2 * LANES), 1)
    ones_col = jnp.logical_or(lane == HEAD_DIM, lane == LANES).astype(F32)
    gv_ref[0] = (proj(C_GV, 2 * LANES) + ones_col).astype(BF16)


def _inproj_call(xs, mod, w, tabs, qg, kg, n_lat):
    b, t, d = xs.shape
    tm, n_lat_tiles = _linear_tiling(n_lat)
    row = lambda width: pl.BlockSpec((1, tm, width), lambda bi, i: (bi, i, 0))
    tab = pl.BlockSpec((tm, LANES), lambda bi, i: (i, 0))
    vec = pl.BlockSpec((1, LANES), lambda bi, i: (0, 0))
    widths = (2 * MLSTM_W, MLSTM_W, MLSTM_W, LANES, DIFF_W, DIFF_W, DIFF_W, GQA_W, GQA_KV_W, 2 * LANES)
    dtypes = (BF16, BF16, BF16, F32, BF16, BF16, BF16, BF16, BF16, BF16)
    return pl.pallas_call(
        _inproj_kernel,
        grid=(b, pl.cdiv(t, tm)),
        in_specs=[row(d),
                  pl.BlockSpec((1, 1, 6, d), lambda bi, i: (bi, jnp.minimum(i // n_lat_tiles, 1), 0, 0)),
                  _resident((d, IN_COLS_PADDED)),
                  tab, tab, tab, vec, vec],
        out_specs=[row(wd) for wd in widths],
        out_shape=[jax.ShapeDtypeStruct((b, t, wd), dt) for wd, dt in zip(widths, dtypes)],
        compiler_params=_cparams("parallel", "parallel"),
        name="in_proj",
    )(xs, mod, w, *tabs, qg, kg)


def _mlstm_kernel(mqk_ref, mv_ref, mo_ref, gate_ref, cw_ref, cb_ref, gb_ref, ng_ref, out_ref,
                  qk_s, h_s, *, n_lat, n_tot):
    lc = ROW_TILE
    t_rows = n_tot * lc
    halo = BF16_SUBLANES

    cw = cw_ref[...]
    cb = cb_ref[...]
    lane_qk = lax.broadcasted_iota(jnp.int32, (1, 2 * MLSTM_W), 1)
    qscale = jnp.where(lane_qk < MLSTM_W, HEAD_DIM ** -0.5, 1.0)

    def conv_body(c, carry):
        r0 = pl.multiple_of(c * lc, lc)
        main = mqk_ref[0, pl.ds(r0, lc), :].astype(F32)
        ts = pl.multiple_of(jnp.maximum(r0 - halo, 0), halo)
        bs = pl.multiple_of(jnp.minimum(r0 + lc, t_rows - halo), halo)
        top_ok = jnp.logical_and(c != 0, c != n_lat).astype(F32)
        bot_ok = jnp.logical_and(c != n_lat - 1, c != n_tot - 1).astype(F32)
        top = mqk_ref[0, pl.ds(ts, halo), :].astype(F32) * top_ok
        bot = mqk_ref[0, pl.ds(bs, halo), :].astype(F32) * bot_ok
        win = jnp.concatenate([top, main, bot], axis=0)
        n = lc + 2 * halo
        y = win * cw[CONV_W // 2:CONV_W // 2 + 1]
        for j in range(CONV_W):
            if j != CONV_W // 2:
                y = y + pltpu.roll(win, (CONV_W // 2 - j) % n, 0) * cw[j:j + 1]
        y = y[halo:halo + lc] + cb
        qk_s[pl.ds(r0, lc), :] = (y * jax.nn.sigmoid(y) * qscale).astype(BF16)
        return carry

    lax.fori_loop(0, n_tot, conv_body, 0)

    row_i = lax.broadcasted_iota(jnp.int32, (lc, lc), 0)
    col_i = lax.broadcasted_iota(jnp.int32, (lc, lc), 1)
    masks_t = (row_i <= col_i, row_i >= col_i)
    tri_row = (masks_t[0].astype(BF16), masks_t[1].astype(BF16))
    tri_col = (tri_row[1], tri_row[0])
    sub = lax.broadcasted_iota(jnp.int32, (LANES, lc), 0)
    sels = (sub < HEAD_DIM, sub >= HEAD_DIM)
    one_rows = ((sub == HEAD_DIM).astype(BF16), (sub == 0).astype(BF16))
    den_row = (HEAD_DIM, 0)
    gb = gb_ref[...]
    h_s[...] = jnp.zeros_like(h_s)
    n_streams = 2 * MLSTM_HEADS

    def step(j, carry):
        cns, ms = carry
        new_cns, new_ms = list(cns), list(ms)
        for d in range(2):
            c = (j + n_lat) % n_tot if d == 0 else n_tot - 1 - j
            r0 = pl.multiple_of(c * lc, lc)
            gates = gate_ref[0, pl.ds(r0, lc), :] + gb
            gates_t = gates.T
            cum = _dot_split_rhs(tri_col[d], jax.nn.log_sigmoid(gates))
            cum_t = _dot_split_lhs(jax.nn.log_sigmoid(gates_t), tri_row[d])
            qk = qk_s[pl.ds(r0, lc), :]
            vv = mv_ref[0, pl.ds(r0, lc), :]
            last = lc - 1 if d == 0 else 0
            for p in range(MLSTM_HEADS // 2):
                q_t = qk[:, p * LANES:(p + 1) * LANES].T
                k128 = qk[:, MLSTM_W + p * LANES:MLSTM_W + (p + 1) * LANES]
                v_t = vv[:, p * LANES:(p + 1) * LANES].T
                halves = []
                for e in range(2):
                    hh = 2 * p + e
                    sidx = d * MLSTM_HEADS + hh
                    ci = d * MLSTM_HEADS + hh
                    cf = 2 * MLSTM_HEADS + d * MLSTM_HEADS + hh
                    qm_t = jnp.where(sels[e], q_t, jnp.zeros_like(q_t))
                    v1_t = jnp.where(sels[e], v_t, one_rows[e])
                    i_row = gates_t[ci:ci + 1, :]
                    a_row = cum_t[cf:cf + 1, :]
                    r_col = gates[:, ci:ci + 1] - cum[:, cf:cf + 1]
                    m = ms[sidx]
                    cn_t = cns[sidx]
                    dm = jnp.where(masks_t[d], a_row + r_col, -jnp.inf)
                    inter = a_row + m
                    m_t = jnp.maximum(jnp.max(dm, axis=0, keepdims=True), inter)
                    w_prev = jnp.exp(inter - m_t)
                    sw = (_dot(k128, qm_t) * jnp.exp(dm - m_t)).astype(BF16)
                    nd = _dot(v1_t, sw) + w_prev * _dot(cn_t.astype(BF16), qm_t)
                    den = nd[den_row[e]:den_row[e] + 1, :]
                    halves.append(nd * (1.0 / jnp.maximum(jnp.abs(den), jnp.exp(-m_t))))
                    b_last = cum_t[cf:cf + 1, last:last + 1]
                    logu = b_last - a_row + i_row
                    m_new = jnp.maximum(b_last + m, jnp.max(logu, axis=1, keepdims=True))
                    vu = (v1_t.astype(F32) * jnp.exp(logu - m_new)).astype(BF16)
                    new_cns[sidx] = jnp.exp(b_last + m - m_new) * cn_t + _dot(vu, k128)
                    new_ms[sidx] = m_new
                h_pair = jnp.where(sels[0], halves[0], halves[1]).T
                h_s[pl.ds(r0, lc), p * LANES:(p + 1) * LANES] += h_pair
        return tuple(new_cns), tuple(new_ms)

    init = (tuple(jnp.zeros((LANES, LANES), F32) for _ in range(n_streams)),
            tuple(jnp.zeros((1, 1), F32) for _ in range(n_streams)))
    lax.fori_loop(0, n_tot, step, init, unroll=2)

    head_mean = _head_mean_matrix(MLSTM_W)
    ng = ng_ref[...]

    def out_body(c, carry):
        r0 = pl.multiple_of(c * lc, lc)
        hb = h_s[pl.ds(r0, lc), :]
        hc = hb - _dot_split_lhs(hb, head_mean)
        var = _dot_split_lhs(hc * hc, head_mean)
        o = mo_ref[0, pl.ds(r0, lc), :].astype(F32)
        out_ref[0, pl.ds(r0, lc), :] = (hc * lax.rsqrt(var + LN_EPS) * ng * jax.nn.sigmoid(o)).astype(BF16)
        return carry

    lax.fori_loop(0, n_tot, out_body, 0, unroll=2)


def _mlstm_call(mqk, mv, mo, gate, cw, cb, gb, ng, n_lat):
    b, t, _ = mqk.shape
    n_tot = t // ROW_TILE
    seq = lambda width: pl.BlockSpec((1, t, width), lambda bi: (bi, 0, 0))
    full = lambda a: pl.BlockSpec(a.shape, lambda bi: (0, 0))
    return pl.pallas_call(
        functools.partial(_mlstm_kernel, n_lat=n_lat, n_tot=n_tot),
        grid=(b,),
        in_specs=[seq(2 * MLSTM_W), seq(MLSTM_W), seq(MLSTM_W), seq(LANES),
                  full(cw), full(cb), full(gb), full(ng)],
        out_specs=seq(MLSTM_W),
        out_shape=jax.ShapeDtypeStruct((b, t, MLSTM_W), BF16),
        scratch_shapes=[pltpu.VMEM((t, 2 * MLSTM_W), BF16), pltpu.VMEM((t, MLSTM_W), F32)],
        compiler_params=_cparams("parallel"),
        name="mlstm",
    )(mqk, mv, mo, gate, cw, cb, gb, ng)


def _key_blocks(first, n_rows, tk):
    blocks, r = [], first
    while r < first + n_rows:
        size = min(tk, first + n_rows - r)
        blocks.append((r, size))
        r += size
    return blocks


def _attn_pair_tile(qms, k_ref, v1_fn, blocks):
    ms, accs = [None, None], [None, None]
    for start, size in blocks:
        k = k_ref[0, pl.ds(start, size), :]
        for e in range(2):
            s = _dot_nt(qms[e], k)
            row_max = jnp.max(s, axis=1, keepdims=True)
            m_new = row_max if ms[e] is None else jnp.maximum(ms[e], row_max)
            pv = _dot(jnp.exp2(s - m_new).astype(BF16), v1_fn(e, start, size))
            accs[e] = pv if accs[e] is None else jnp.exp2(ms[e] - m_new) * accs[e] + pv
            ms[e] = m_new
    return accs


def _for_query_tiles(n_lat_rows, n_ctx_rows, tiles, tile_fn):
    tq, tk, unroll = tiles
    tq = min(tq, n_lat_rows)
    lat_blocks = _key_blocks(0, n_lat_rows, tk) + _key_blocks(n_lat_rows, n_ctx_rows, tk)
    ctx_blocks = _key_blocks(n_lat_rows, n_ctx_rows, tk)

    def lat_body(i, carry):
        tile_fn(pl.multiple_of(i * tq, tq), tq, lat_blocks)
        return carry

    lax.fori_loop(0, n_lat_rows // tq, lat_body, 0, unroll=unroll)
    tq_ctx = min(tq, n_ctx_rows)
    for i in range(n_ctx_rows // tq_ctx):
        tile_fn(n_lat_rows + i * tq_ctx, tq_ctx, ctx_blocks)


def _diff_attn_kernel(q_ref, k_ref, v_ref, lam_ref, g_ref, o_ref, *, n_lat_rows, tiles, lam_init):
    dv = 2 * HEAD_DIM
    n_ctx_rows = q_ref.shape[1] - n_lat_rows
    lane = lax.broadcasted_iota(jnp.int32, (1, LANES), 1)
    ones_blk = jnp.broadcast_to((lane == 0).astype(BF16), (tiles[1], LANES))
    lv = lam_ref[0]
    lam = (jnp.exp(jnp.sum(lv[0:1] * lv[1:2], axis=1, keepdims=True))
           - jnp.exp(jnp.sum(lv[2:3] * lv[3:4], axis=1, keepdims=True)) + lam_init)
    gain = g_ref[...] * (1.0 - lam_init)

    def v1_fn(e, start, size):
        return jnp.concatenate([v_ref[0, pl.ds(start, size), :], ones_blk[:size]], axis=1)

    def tile_fn(r0, rows, blocks):
        q = q_ref[0, pl.ds(r0, rows), :]
        zero = jnp.zeros_like(q)
        qms = (jnp.where(lane < HEAD_DIM, q, zero), jnp.where(lane >= HEAD_DIM, q, zero))
        a1, a2 = _attn_pair_tile(qms, k_ref, v1_fn, blocks)
        o = a1[:, :dv] * (1.0 / a1[:, dv:dv + 1]) - lam * (a2[:, :dv] * (1.0 / a2[:, dv:dv + 1]))
        ms = jnp.mean(o * o, axis=1, keepdims=True)
        o_ref[0, pl.ds(r0, rows), :] = (o * lax.rsqrt(ms + LN_EPS) * gain).astype(BF16)

    _for_query_tiles(n_lat_rows, n_ctx_rows, tiles, tile_fn)


def _diff_attn_call(dq, dk, dv, lam_vecs, g, n_lat, lam_init):
    b, t, _ = dq.shape
    seq = pl.BlockSpec((1, t, LANES), lambda bi, h: (bi, 0, h))
    return pl.pallas_call(
        functools.partial(_diff_attn_kernel, n_lat_rows=n_lat * ROW_TILE, tiles=DIFF_TILES,
                          lam_init=lam_init),
        grid=(b, DIFF_HEADS),
        in_specs=[seq, seq, seq,
                  pl.BlockSpec((1, 4, HEAD_DIM), lambda bi, h: (0, 0, 0)),
                  pl.BlockSpec((1, LANES), lambda bi, h: (0, 0))],
        out_specs=seq,
        out_shape=jax.ShapeDtypeStruct((b, t, DIFF_W), BF16),
        compiler_params=_cparams("parallel", "parallel"),
        name="diff_attn",
    )(dq, dk, dv, lam_vecs, g)


def _gqa_attn_kernel(q_ref, k_ref, v_ref, o_ref, *, n_lat_rows, tiles):
    n_ctx_rows = q_ref.shape[1] - n_lat_rows
    lane = lax.broadcasted_iota(jnp.int32, (1, LANES), 1)
    first_half = lane < HEAD_DIM
    den_lane = (HEAD_DIM, 0)

    def v1_fn(e, start, size):
        return v_ref[0, pl.ds(start, size), e * LANES:(e + 1) * LANES]

    def tile_fn(r0, rows, blocks):
        q = q_ref[0, pl.ds(r0, rows), :]
        zero = jnp.zeros_like(q)
        qms = (jnp.where(first_half, q, zero), jnp.where(first_half, zero, q))
        accs = _attn_pair_tile(qms, k_ref, v1_fn, blocks)
        halves = [acc * (1.0 / acc[:, den_lane[e]:den_lane[e] + 1]) for e, acc in enumerate(accs)]
        o_ref[0, pl.ds(r0, rows), :] = jnp.where(first_half, halves[0], halves[1]).astype(BF16)

    _for_query_tiles(n_lat_rows, n_ctx_rows, tiles, tile_fn)


def _gqa_attn_call(gq, gk, gv, n_lat):
    b, t, _ = gq.shape
    pair = pl.BlockSpec((1, t, LANES), lambda bi, p: (bi, 0, p))
    return pl.pallas_call(
        functools.partial(_gqa_attn_kernel, n_lat_rows=n_lat * ROW_TILE, tiles=GQA_TILES),
        grid=(b, GQA_W // LANES),
        in_specs=[pair,
                  pl.BlockSpec((1, t, GQA_KV_W), lambda bi, p: (bi, 0, 0)),
                  pl.BlockSpec((1, t, 2 * LANES), lambda bi, p: (bi, 0, 0))],
        out_specs=pair,
        out_shape=jax.ShapeDtypeStruct((b, t, GQA_W), BF16),
        compiler_params=_cparams("parallel", "arbitrary"),
        name="gqa_attn",
    )(gq, gk, gv)


def _post_kernel(a_ref, bd_ref, cg_ref, x_ref, mod_ref, w_ref, g1_ref, b1_ref,
                 wi_ref, wo_ref, g2_ref, b2_ref, o_ref, *, alpha, d_ff, chunk):
    mod = mod_ref[0, 0]
    y = (_dot(a_ref[0], w_ref[0:MLSTM_W, :])
         + _dot(bd_ref[0], w_ref[MLSTM_W:MLSTM_W + DIFF_W, :])
         + _dot(cg_ref[0], w_ref[MLSTM_W + DIFF_W:, :]))
    x = _layer_norm(alpha * x_ref[0] + mod[2:3] * y, g1_ref[...], b1_ref[...])
    xm = (x * (1.0 + mod[4:5]) + mod[3:4]).astype(BF16)
    acc = jnp.zeros(x.shape, F32)
    for c in range(d_ff // chunk):
        gate = _dot(xm, wi_ref[:, c * chunk:(c + 1) * chunk])
        up = _dot(xm, wi_ref[:, d_ff + c * chunk:d_ff + (c + 1) * chunk])
        act = (gate * jax.nn.sigmoid(gate) * up).astype(BF16)
        acc = acc + _dot(act, wo_ref[c * chunk:(c + 1) * chunk, :])
    o_ref[0] = _layer_norm(alpha * x + mod[5:6] * acc, g2_ref[...], b2_ref[...])


def _post_call(a, bd, cg, xs, mod, w, g1, b1, wi, wo, g2, b2, n_lat, n_rows, alpha):
    b, _, d = xs.shape
    d_ff = wo.shape[0]
    tm, n_lat_tiles = _linear_tiling(n_lat)
    row = lambda width: pl.BlockSpec((1, tm, width), lambda bi, i: (bi, i, 0))
    vec = pl.BlockSpec((1, d), lambda bi, i: (0, 0))
    full = lambda arr: _resident(arr.shape)
    return pl.pallas_call(
        functools.partial(_post_kernel, alpha=alpha, d_ff=d_ff, chunk=2 * LANES),
        grid=(b, pl.cdiv(n_rows, tm)),
        in_specs=[row(MLSTM_W), row(DIFF_W), row(GQA_W), row(d),
                  pl.BlockSpec((1, 1, 6, d), lambda bi, i: (bi, jnp.minimum(i // n_lat_tiles, 1), 0, 0)),
                  full(w), vec, vec, full(wi), full(wo), vec, vec],
        out_specs=row(d),
        out_shape=jax.ShapeDtypeStruct((b, n_rows, d), F32),
        compiler_params=_cparams("parallel", "parallel"),
        name="out_proj_ffn",
    )(a, bd, cg, xs, mod, w, g1, b1, wi, wo, g2, b2)


def _prep_w_in(w):
    zeros = lambda n: jnp.zeros((w.shape[0], n), w.dtype)
    o_gate = 4 * MLSTM_W
    o_dq = o_gate + 4 * MLSTM_HEADS
    o_gq = o_dq + 3 * DIFF_W
    o_gk = o_gq + GQA_W
    o_gv = o_gk + GQA_KV_W
    gq = w[:, o_gq:o_gk].reshape(-1, 2, 2, HEAD_DIM).transpose(0, 2, 1, 3).reshape(-1, GQA_W)
    gv = w[:, o_gv:o_gv + GQA_KV_W]
    cols = [w[:, :o_gate], w[:, o_dq:o_gq], gq, w[:, o_gk:o_gv],
            gv[:, :HEAD_DIM], zeros(LANES), gv[:, HEAD_DIM:],
            w[:, o_gate:o_dq], zeros(LANES - 4 * MLSTM_HEADS)]
    out = jnp.concatenate(cols, axis=1)
    assert out.shape[1] == IN_COLS_PADDED
    return out.astype(BF16)


def _prep_w_out(w):
    o_c = MLSTM_W + DIFF_W
    gq = w[o_c:].reshape(2, 2, HEAD_DIM, -1).transpose(1, 0, 2, 3).reshape(GQA_W, -1)
    return jnp.concatenate([w[:o_c], gq], axis=0).astype(BF16)


def _rope_tables(s, n_ctx_rows):
    pos = jnp.arange(s, dtype=jnp.int32)
    row = (pos // GRID_W).astype(F32)
    col = (pos % GRID_W).astype(F32)
    n_freq = HEAD_DIM // 4
    inv = ROPE_THETA ** (-jnp.arange(n_freq, dtype=F32) / n_freq)
    ar = row[:, None] * inv
    ac = col[:, None] * inv
    ang = jnp.concatenate([ar, ar, ac, ac], axis=-1)
    ang = jnp.concatenate([ang, ang], axis=-1)
    first = (jnp.arange(LANES) % (2 * ROPE_HALF)) < ROPE_HALF
    cos, sin = jnp.cos(ang), jnp.sin(ang)
    sa = jnp.where(first, -sin, 0.0)
    sb = jnp.where(first, 0.0, sin)
    pad = lambda a, v: jnp.concatenate([a, jnp.full((n_ctx_rows, LANES), v, F32)], axis=0)
    return pad(cos, 1.0), pad(sa, 0.0), pad(sb, 0.0)


def kernel(x, c, ctx, c_ctx, w_ada, b_ada, w_in, mlstm_conv_w, mlstm_conv_b, mlstm_gate_b, mlstm_norm_g, diff_lambda, diff_norm_g, gqa_q_norm_g, gqa_k_norm_g, w_out, ln1_g, ln1_b, w_ffn_in, w_ffn_out, ln2_g, ln2_b):
    b, s, d = x.shape
    n_ctx_rows = ctx.shape[1]
    depth = w_in.shape[0]
    assert s % ROW_TILE == 0 and n_ctx_rows % ROW_TILE == 0 and s % GRID_W == 0
    n_lat = s // ROW_TILE
    t = s + n_ctx_rows
    alpha = (2 * depth) ** 0.25

    ada_rows = -(-(b + 1) // 8) * 8
    cvec = jnp.concatenate([c, c_ctx[None, :], jnp.zeros((ada_rows - b - 1, d), F32)], axis=0)
    mod_all = _ada_call(cvec, w_ada, b_ada)
    mod_lat = mod_all[:, :b].reshape(depth, b, 1, 6, d)
    mod_ctx = jnp.broadcast_to(mod_all[:, b].reshape(depth, 1, 1, 6, d), (depth, b, 1, 6, d))
    mod_all = jnp.concatenate([mod_lat, mod_ctx], axis=2)

    tabs = _rope_tables(s, n_ctx_rows)
    tile2 = lambda v, n: jnp.tile(v, n).reshape(1, -1)
    xs = jnp.concatenate([x, ctx], axis=1)

    for l in range(depth):
        last = l == depth - 1
        lam_init = 0.8 - 0.6 * math.exp(-0.3 * l)
        mod = mod_all[l]
        (mqk, mv, mo, gate, dq, dk, dv, gq, gk, gv) = _inproj_call(
            xs, mod, _prep_w_in(w_in[l]), tabs,
            tile2(gqa_q_norm_g[l], LANES // HEAD_DIM), tile2(gqa_k_norm_g[l], LANES // HEAD_DIM), n_lat)
        cw = jnp.concatenate([mlstm_conv_w[l], jnp.zeros((8 - CONV_W, 2 * MLSTM_W), F32)], axis=0)
        gb = jnp.concatenate([mlstm_gate_b[l], jnp.zeros((LANES - 4 * MLSTM_HEADS,), F32)]).reshape(1, LANES)
        a = _mlstm_call(mqk, mv, mo, gate, cw, mlstm_conv_b[l].reshape(1, -1), gb,
                        mlstm_norm_g[l].reshape(1, -1), n_lat)
        bd = _diff_attn_call(dq, dk, dv, diff_lambda[l][None], diff_norm_g[l].reshape(1, -1), n_lat, lam_init)
        cg = _gqa_attn_call(gq, gk, gv, n_lat)
        n_rows = s if last else t
        xs = _post_call(a, bd, cg, xs, mod, _prep_w_out(w_out[l]),
                        ln1_g[l].reshape(1, -1), ln1_b[l].reshape(1, -1),
                        w_ffn_in[l].astype(BF16), w_ffn_out[l].astype(BF16),
                        ln2_g[l].reshape(1, -1), ln2_b[l].reshape(1, -1), n_lat, n_rows, alpha)
    return xs
```

```python
import functools
import math

import jax
import jax.numpy as jnp
from jax import lax
from jax.experimental import pallas as pl
from jax.experimental.pallas import tpu as pltpu

F32 = jnp.float32
BF16 = jnp.bfloat16

HEAD_DIM = 64
MLSTM_HEADS = 4
DIFF_HEADS = 4
GQA_HEADS = 4
GQA_KV_HEADS = 2
GRID_W = 64
CONV_W = 5
ROPE_THETA = 10000.0
LN_EPS = 1e-5

MLSTM_W = MLSTM_HEADS * HEAD_DIM
DIFF_W = DIFF_HEADS * 2 * HEAD_DIM
GQA_W = GQA_HEADS * HEAD_DIM
GQA_KV_W = GQA_KV_HEADS * HEAD_DIM

LANES = 128
BF16_SUBLANES = 16
VMEM_LIMIT_BYTES = 56 * 1024 * 1024

ROW_TILE = 256
LINEAR_TILE = 512
ROPE_HALF = HEAD_DIM // 4
DIFF_TILES = (512, 1024, 2)
GQA_TILES = (256, 4096, 2)
ATTN_Q_SCALE = HEAD_DIM ** -0.5 * math.log2(math.e)

C_MQK, C_MV, C_MO = 0, 512, 768
C_DQ, C_DK, C_DV = 1024, 1536, 2048
C_GQ, C_GK, C_GV, C_GATE = 2560, 2816, 2944, 3200
IN_COLS_PADDED = 3328


def _cparams(*sem):
    return pltpu.CompilerParams(dimension_semantics=sem, vmem_limit_bytes=VMEM_LIMIT_BYTES)


def _resident(shape):
    return pl.BlockSpec(shape, lambda *_: (0,) * len(shape), pipeline_mode=pl.Buffered(1))


class _FlatTiling:
    def __init__(self, b, t, n_lat):
        self.tm = LINEAR_TILE
        assert LINEAR_TILE == 2 * ROW_TILE and (b * t) % self.tm == 0
        self.n_tiles = b * t // self.tm
        self.n_tot = t // ROW_TILE
        self.n_lat = n_lat

    def unit_in_batch(self, i, h):
        return (2 * i + h) % self.n_tot

    def mod_spec(self, h, d):
        return pl.BlockSpec(
            (1, 1, 6, d),
            lambda i: ((2 * i + h) // self.n_tot, jnp.minimum(self.unit_in_batch(i, h) // self.n_lat, 1), 0, 0))


def _dot(a, b):
    return jnp.dot(a, b, preferred_element_type=F32)


def _dot_nt(a, b):
    return lax.dot_general(a, b, (((1,), (1,)), ((), ())), preferred_element_type=F32)


def _split_bf16(a):
    hi = a.astype(BF16)
    lo = (a - hi.astype(F32)).astype(BF16)
    return hi, lo


def _dot_split_lhs(a, m):
    hi, lo = _split_bf16(a)
    return _dot(hi, m) + _dot(lo, m)


def _dot_split_rhs(m, a):
    hi, lo = _split_bf16(a)
    return _dot(m, hi) + _dot(m, lo)


def _head_mean_matrix(width):
    r = lax.broadcasted_iota(jnp.int32, (width, width), 0) // HEAD_DIM
    c = lax.broadcasted_iota(jnp.int32, (width, width), 1) // HEAD_DIM
    return jnp.where(r == c, 1.0 / HEAD_DIM, 0.0).astype(BF16)


def _layer_norm(z, g, b):
    mu = jnp.mean(z, axis=-1, keepdims=True)
    zc = z - mu
    var = jnp.mean(zc * zc, axis=-1, keepdims=True)
    return zc * lax.rsqrt(var + LN_EPS) * g + b


def _ada_kernel(c_ref, w_ref, b_ref, o_ref):
    c = c_ref[...]
    o_ref[0] = _dot(c * jax.nn.sigmoid(c), w_ref[0]) + b_ref[0]


def _ada_call(cvec, w_ada, b_ada):
    depth, d, six_d = w_ada.shape
    rows = cvec.shape[0]
    bn = 1024
    return pl.pallas_call(
        _ada_kernel,
        grid=(depth, six_d // bn),
        in_specs=[pl.BlockSpec((rows, d), lambda l, j: (0, 0)),
                  pl.BlockSpec((1, d, bn), lambda l, j: (l, 0, j)),
                  pl.BlockSpec((1, 1, bn), lambda l, j: (l, 0, j))],
        out_specs=pl.BlockSpec((1, rows, bn), lambda l, j: (l, 0, j)),
        out_shape=jax.ShapeDtypeStruct((depth, rows, six_d), F32),
        compiler_params=_cparams("parallel", "parallel"),
        name="adaln_mod",
    )(cvec, w_ada, b_ada.reshape(depth, 1, six_d))


def _per_half(fn, mods, *arrays):
    half = arrays[0].shape[0] // 2
    return jnp.concatenate([fn(m, *(a[i * half:(i + 1) * half] for a in arrays))
                            for i, m in enumerate(mods)], axis=0)


def _inproj_kernel(x_ref, moda_ref, modb_ref, w_ref, cosa_ref, cosb_ref, saa_ref, sab_ref,
                   sba_ref, sbb_ref, qg_ref, kg_ref,
                   mqk_ref, mv_ref, mo_ref, gate_ref, dq_ref, dk_ref, dv_ref,
                   gq_ref, gk_ref, gv_ref):
    mods = (moda_ref[0, 0], modb_ref[0, 0])
    xm = _per_half(lambda m, x: x * (1.0 + m[1:2]) + m[0:1], mods, x_ref[0]).astype(BF16)
    cos = jnp.concatenate([cosa_ref[...], cosb_ref[...]], axis=0)
    sa = jnp.concatenate([saa_ref[...], sab_ref[...]], axis=0)
    sb = jnp.concatenate([sba_ref[...], sbb_ref[...]], axis=0)
    scale = ATTN_Q_SCALE

    def proj(lo, n):
        return _dot(xm, w_ref[:, lo:lo + n])

    def rope(a):
        return (a * cos + pltpu.roll(a, LANES - ROPE_HALF, 1) * sa
                + pltpu.roll(a, ROPE_HALF, 1) * sb)

    mqk_ref[0] = proj(C_MQK, 2 * MLSTM_W).astype(BF16)
    mv_ref[0] = proj(C_MV, MLSTM_W).astype(BF16)
    mo_ref[0] = proj(C_MO, MLSTM_W).astype(BF16)
    gate_ref[0] = proj(C_GATE, LANES)

    acc = proj(C_DQ, DIFF_W)
    for j in range(DIFF_W // LANES):
        dq_ref[0, :, j * LANES:(j + 1) * LANES] = (
            rope(acc[:, j * LANES:(j + 1) * LANES]) * scale).astype(BF16)
    acc = proj(C_DK, DIFF_W)
    for j in range(DIFF_W // LANES):
        dk_ref[0, :, j * LANES:(j + 1) * LANES] = rope(acc[:, j * LANES:(j + 1) * LANES]).astype(BF16)
    dv_ref[0] = proj(C_DV, DIFF_W).astype(BF16)

    head_mean = _head_mean_matrix(LANES)

    def rms(a, g):
        ms = _dot_split_lhs(a * a, head_mean)
        return a * lax.rsqrt(ms + LN_EPS) * g

    acc = proj(C_GQ, GQA_W)
    qg = qg_ref[...]
    for j in range(GQA_W // LANES):
        gq_ref[0, :, j * LANES:(j + 1) * LANES] = (
            rope(rms(acc[:, j * LANES:(j + 1) * LANES], qg)) * scale).astype(BF16)
    gk_ref[0] = rope(rms(proj(C_GK, GQA_KV_W), kg_ref[...])).astype(BF16)

    lane = lax.broadcasted_iota(jnp.int32, (1, 2 * LANES), 1)
    ones_col = jnp.logical_or(lane == HEAD_DIM, lane == LANES).astype(F32)
    gv_ref[0] = (proj(C_GV, 2 * LANES) + ones_col).astype(BF16)


def _inproj_call(xs, mod, w, tabs, qg, kg, n_lat):
    b, t, d = xs.shape
    flat = _FlatTiling(b, t, n_lat)
    row = lambda width: pl.BlockSpec((1, flat.tm, width), lambda i: (0, i, 0))
    tab = lambda h: pl.BlockSpec((ROW_TILE, LANES), lambda i: (flat.unit_in_batch(i, h), 0))
    vec = pl.BlockSpec((1, LANES), lambda i: (0, 0))
    widths = (2 * MLSTM_W, MLSTM_W, MLSTM_W, LANES, DIFF_W, DIFF_W, DIFF_W, GQA_W, GQA_KV_W, 2 * LANES)
    dtypes = (BF16, BF16, BF16, F32, BF16, BF16, BF16, BF16, BF16, BF16)
    outs = pl.pallas_call(
        _inproj_kernel,
        grid=(flat.n_tiles,),
        in_specs=[row(d), flat.mod_spec(0, d), flat.mod_spec(1, d),
                  _resident((d, IN_COLS_PADDED)),
                  tab(0), tab(1), tab(0), tab(1), tab(0), tab(1), vec, vec],
        out_specs=[row(wd) for wd in widths],
        out_shape=[jax.ShapeDtypeStruct((1, b * t, wd), dt) for wd, dt in zip(widths, dtypes)],
        compiler_params=_cparams("parallel"),
        name="in_proj",
    )(xs.reshape(1, b * t, d), mod, mod, w, tabs[0], tabs[0], tabs[1], tabs[1], tabs[2], tabs[2], qg, kg)
    return [o.reshape(b, t, -1) for o in outs]


def _mlstm_kernel(mqk_ref, mv_ref, mo_ref, gate_ref, cw_ref, cb_ref, gb_ref, ng_ref, out_ref,
                  qk_s, h_s, *, n_lat, n_tot):
    lc = ROW_TILE
    t_rows = n_tot * lc
    halo = BF16_SUBLANES

    cw = cw_ref[...]
    cb = cb_ref[...]
    lane_qk = lax.broadcasted_iota(jnp.int32, (1, 2 * MLSTM_W), 1)
    qscale = jnp.where(lane_qk < MLSTM_W, HEAD_DIM ** -0.5, 1.0)

    def conv_body(c, carry):
        r0 = pl.multiple_of(c * lc, lc)
        main = mqk_ref[0, pl.ds(r0, lc), :].astype(F32)
        ts = pl.multiple_of(jnp.maximum(r0 - halo, 0), halo)
        bs = pl.multiple_of(jnp.minimum(r0 + lc, t_rows - halo), halo)
        top_ok = jnp.logical_and(c != 0, c != n_lat).astype(F32)
        bot_ok = jnp.logical_and(c != n_lat - 1, c != n_tot - 1).astype(F32)
        top = mqk_ref[0, pl.ds(ts, halo), :].astype(F32) * top_ok
        bot = mqk_ref[0, pl.ds(bs, halo), :].astype(F32) * bot_ok
        win = jnp.concatenate([top, main, bot], axis=0)
        n = lc + 2 * halo
        y = win * cw[CONV_W // 2:CONV_W // 2 + 1]
        for j in range(CONV_W):
            if j != CONV_W // 2:
                y = y + pltpu.roll(win, (CONV_W // 2 - j) % n, 0) * cw[j:j + 1]
        y = y[halo:halo + lc] + cb
        qk_s[pl.ds(r0, lc), :] = (y * jax.nn.sigmoid(y) * qscale).astype(BF16)
        return carry

    lax.fori_loop(0, n_tot, conv_body, 0)

    row_i = lax.broadcasted_iota(jnp.int32, (lc, lc), 0)
    col_i = lax.broadcasted_iota(jnp.int32, (lc, lc), 1)
    masks_t = (row_i <= col_i, row_i >= col_i)
    tri_row = (masks_t[0].astype(BF16), masks_t[1].astype(BF16))
    tri_col = (tri_row[1], tri_row[0])
    sub = lax.broadcasted_iota(jnp.int32, (LANES, lc), 0)
    sels = (sub < HEAD_DIM, sub >= HEAD_DIM)
    one_rows = ((sub == HEAD_DIM).astype(BF16), (sub == 0).astype(BF16))
    den_row = (HEAD_DIM, 0)
    gb = gb_ref[...]
    h_s[...] = jnp.zeros_like(h_s)
    n_streams = 2 * MLSTM_HEADS

    def step(j, carry):
        cns, ms = carry
        new_cns, new_ms = list(cns), list(ms)
        for d in range(2):
            c = (j + n_lat) % n_tot if d == 0 else n_tot - 1 - j
            r0 = pl.multiple_of(c * lc, lc)
            gates = gate_ref[0, pl.ds(r0, lc), :] + gb
            gates_t = gates.T
            cum = _dot_split_rhs(tri_col[d], jax.nn.log_sigmoid(gates))
            cum_t = _dot_split_lhs(jax.nn.log_sigmoid(gates_t), tri_row[d])
            qk = qk_s[pl.ds(r0, lc), :]
            vv = mv_ref[0, pl.ds(r0, lc), :]
            last = lc - 1 if d == 0 else 0
            for p in range(MLSTM_HEADS // 2):
                q_t = qk[:, p * LANES:(p + 1) * LANES].T
                k128 = qk[:, MLSTM_W + p * LANES:MLSTM_W + (p + 1) * LANES]
                v_t = vv[:, p * LANES:(p + 1) * LANES].T
                halves = []
                for e in range(2):
                    hh = 2 * p + e
                    sidx = d * MLSTM_HEADS + hh
                    ci = d * MLSTM_HEADS + hh
                    cf = 2 * MLSTM_HEADS + d * MLSTM_HEADS + hh
                    qm_t = jnp.where(sels[e], q_t, jnp.zeros_like(q_t))
                    v1_t = jnp.where(sels[e], v_t, one_rows[e])
                    i_row = gates_t[ci:ci + 1, :]
                    a_row = cum_t[cf:cf + 1, :]
                    r_col = gates[:, ci:ci + 1] - cum[:, cf:cf + 1]
                    m = ms[sidx]
                    cn_t = cns[sidx]
                    dm = jnp.where(masks_t[d], a_row + r_col, -jnp.inf)
                    inter = a_row + m
                    m_t = jnp.maximum(jnp.max(dm, axis=0, keepdims=True), inter)
                    w_prev = jnp.exp(inter - m_t)
                    sw = (_dot(k128, qm_t) * jnp.exp(dm - m_t)).astype(BF16)
                    nd = _dot(v1_t, sw) + w_prev * _dot(cn_t.astype(BF16), qm_t)
                    den = nd[den_row[e]:den_row[e] + 1, :]
                    halves.append(nd * (1.0 / jnp.maximum(jnp.abs(den), jnp.exp(-m_t))))
                    b_last = cum_t[cf:cf + 1, last:last + 1]
                    logu = b_last - a_row + i_row
                    m_new = jnp.maximum(b_last + m, jnp.max(logu, axis=1, keepdims=True))
                    vu = (v1_t.astype(F32) * jnp.exp(logu - m_new)).astype(BF16)
                    new_cns[sidx] = jnp.exp(b_last + m - m_new) * cn_t + _dot(vu, k128)
                    new_ms[sidx] = m_new
                h_pair = jnp.where(sels[0], halves[0], halves[1]).T
                h_s[pl.ds(r0, lc), p * LANES:(p + 1) * LANES] += h_pair
        return tuple(new_cns), tuple(new_ms)

    init = (tuple(jnp.zeros((LANES, LANES), F32) for _ in range(n_streams)),
            tuple(jnp.zeros((1, 1), F32) for _ in range(n_streams)))
    lax.fori_loop(0, n_tot, step, init, unroll=2)

    head_mean = _head_mean_matrix(MLSTM_W)
    ng = ng_ref[...]

    def out_body(c, carry):
        r0 = pl.multiple_of(c * lc, lc)
        hb = h_s[pl.ds(r0, lc), :]
        hc = hb - _dot_split_lhs(hb, head_mean)
        var = _dot_split_lhs(hc * hc, head_mean)
        o = mo_ref[0, pl.ds(r0, lc), :].astype(F32)
        out_ref[0, pl.ds(r0, lc), :] = (hc * lax.rsqrt(var + LN_EPS) * ng * jax.nn.sigmoid(o)).astype(BF16)
        return carry

    lax.fori_loop(0, n_tot, out_body, 0, unroll=2)


def _mlstm_call(mqk, mv, mo, gate, cw, cb, gb, ng, n_lat):
    b, t, _ = mqk.shape
    n_tot = t // ROW_TILE
    seq = lambda width: pl.BlockSpec((1, t, width), lambda bi: (bi, 0, 0))
    full = lambda a: pl.BlockSpec(a.shape, lambda bi: (0, 0))
    return pl.pallas_call(
        functools.partial(_mlstm_kernel, n_lat=n_lat, n_tot=n_tot),
        grid=(b,),
        in_specs=[seq(2 * MLSTM_W), seq(MLSTM_W), seq(MLSTM_W), seq(LANES),
                  full(cw), full(cb), full(gb), full(ng)],
        out_specs=seq(MLSTM_W),
        out_shape=jax.ShapeDtypeStruct((b, t, MLSTM_W), BF16),
        scratch_shapes=[pltpu.VMEM((t, 2 * MLSTM_W), BF16), pltpu.VMEM((t, MLSTM_W), F32)],
        compiler_params=_cparams("parallel"),
        name="mlstm",
    )(mqk, mv, mo, gate, cw, cb, gb, ng)


def _key_blocks(first, n_rows, tk):
    blocks, r = [], first
    while r < first + n_rows:
        size = min(tk, first + n_rows - r)
        blocks.append((r, size))
        r += size
    return blocks


def _attn_pair_tile(qms, k_ref, v1_fn, blocks):
    ms, accs = [None, None], [None, None]
    for start, size in blocks:
        k = k_ref[0, pl.ds(start, size), :]
        for e in range(2):
            s = _dot_nt(qms[e], k)
            row_max = jnp.max(s, axis=1, keepdims=True)
            m_new = row_max if ms[e] is None else jnp.maximum(ms[e], row_max)
            pv = _dot(jnp.exp2(s - m_new).astype(BF16), v1_fn(e, start, size))
            accs[e] = pv if accs[e] is None else jnp.exp2(ms[e] - m_new) * accs[e] + pv
            ms[e] = m_new
    return accs


def _for_query_tiles(n_lat_rows, n_ctx_rows, tiles, tile_fn):
    tq, tk, unroll = tiles
    tq = min(tq, n_lat_rows)
    lat_blocks = _key_blocks(0, n_lat_rows, tk) + _key_blocks(n_lat_rows, n_ctx_rows, tk)
    ctx_blocks = _key_blocks(n_lat_rows, n_ctx_rows, tk)

    def lat_body(i, carry):
        tile_fn(pl.multiple_of(i * tq, tq), tq, lat_blocks)
        return carry

    lax.fori_loop(0, n_lat_rows // tq, lat_body, 0, unroll=unroll)
    tq_ctx = min(tq, n_ctx_rows)
    for i in range(n_ctx_rows // tq_ctx):
        tile_fn(n_lat_rows + i * tq_ctx, tq_ctx, ctx_blocks)


def _diff_attn_kernel(q_ref, k_ref, v_ref, lam_ref, g_ref, o_ref, *, n_lat_rows, tiles, lam_init):
    dv = 2 * HEAD_DIM
    n_ctx_rows = q_ref.shape[1] - n_lat_rows
    lane = lax.broadcasted_iota(jnp.int32, (1, LANES), 1)
    ones_blk = jnp.broadcast_to((lane == 0).astype(BF16), (tiles[1], LANES))
    lv = lam_ref[0]
    lam = (jnp.exp(jnp.sum(lv[0:1] * lv[1:2], axis=1, keepdims=True))
           - jnp.exp(jnp.sum(lv[2:3] * lv[3:4], axis=1, keepdims=True)) + lam_init)
    gain = g_ref[...] * (1.0 - lam_init)

    def v1_fn(e, start, size):
        return jnp.concatenate([v_ref[0, pl.ds(start, size), :], ones_blk[:size]], axis=1)

    def tile_fn(r0, rows, blocks):
        q = q_ref[0, pl.ds(r0, rows), :]
        zero = jnp.zeros_like(q)
        qms = (jnp.where(lane < HEAD_DIM, q, zero), jnp.where(lane >= HEAD_DIM, q, zero))
        a1, a2 = _attn_pair_tile(qms, k_ref, v1_fn, blocks)
        o = a1[:, :dv] * (1.0 / a1[:, dv:dv + 1]) - lam * (a2[:, :dv] * (1.0 / a2[:, dv:dv + 1]))
        ms = jnp.mean(o * o, axis=1, keepdims=True)
        o_ref[0, pl.ds(r0, rows), :] = (o * lax.rsqrt(ms + LN_EPS) * gain).astype(BF16)

    _for_query_tiles(n_lat_rows, n_ctx_rows, tiles, tile_fn)


def _diff_attn_call(dq, dk, dv, lam_vecs, g, n_lat, lam_init):
    b, t, _ = dq.shape
    seq = pl.BlockSpec((1, t, LANES), lambda bi, h: (bi, 0, h))
    return pl.pallas_call(
        functools.partial(_diff_attn_kernel, n_lat_rows=n_lat * ROW_TILE, tiles=DIFF_TILES,
                          lam_init=lam_init),
        grid=(b, DIFF_HEADS),
        in_specs=[seq, seq, seq,
                  pl.BlockSpec((1, 4, HEAD_DIM), lambda bi, h: (0, 0, 0)),
                  pl.BlockSpec((1, LANES), lambda bi, h: (0, 0))],
        out_specs=seq,
        out_shape=jax.ShapeDtypeStruct((b, t, DIFF_W), BF16),
        compiler_params=_cparams("parallel", "parallel"),
        name="diff_attn",
    )(dq, dk, dv, lam_vecs, g)


def _gqa_attn_kernel(q_ref, k_ref, v_ref, o_ref, *, n_lat_rows, tiles):
    n_ctx_rows = q_ref.shape[1] - n_lat_rows
    lane = lax.broadcasted_iota(jnp.int32, (1, LANES), 1)
    first_half = lane < HEAD_DIM
    den_lane = (HEAD_DIM, 0)

    def v1_fn(e, start, size):
        return v_ref[0, pl.ds(start, size), e * LANES:(e + 1) * LANES]

    def tile_fn(r0, rows, blocks):
        q = q_ref[0, pl.ds(r0, rows), :]
        zero = jnp.zeros_like(q)
        qms = (jnp.where(first_half, q, zero), jnp.where(first_half, zero, q))
        accs = _attn_pair_tile(qms, k_ref, v1_fn, blocks)
        halves = [acc * (1.0 / acc[:, den_lane[e]:den_lane[e] + 1]) for e, acc in enumerate(accs)]
        o_ref[0, pl.ds(r0, rows), :] = jnp.where(first_half, halves[0], halves[1]).astype(BF16)

    _for_query_tiles(n_lat_rows, n_ctx_rows, tiles, tile_fn)


def _gqa_attn_call(gq, gk, gv, n_lat):
    b, t, _ = gq.shape
    pair = pl.BlockSpec((1, t, LANES), lambda bi, p: (bi, 0, p))
    return pl.pallas_call(
        functools.partial(_gqa_attn_kernel, n_lat_rows=n_lat * ROW_TILE, tiles=GQA_TILES),
        grid=(b, GQA_W // LANES),
        in_specs=[pair,
                  pl.BlockSpec((1, t, GQA_KV_W), lambda bi, p: (bi, 0, 0)),
                  pl.BlockSpec((1, t, 2 * LANES), lambda bi, p: (bi, 0, 0))],
        out_specs=pair,
        out_shape=jax.ShapeDtypeStruct((b, t, GQA_W), BF16),
        compiler_params=_cparams("parallel", "arbitrary"),
        name="gqa_attn",
    )(gq, gk, gv)


def _post_kernel(a_ref, bd_ref, cg_ref, x_ref, moda_ref, modb_ref, w_ref, g1_ref, b1_ref,
                 wi_ref, wo_ref, g2_ref, b2_ref, o_ref, *, alpha, d_ff, chunk):
    mods = (moda_ref[0, 0], modb_ref[0, 0])
    y = (_dot(a_ref[0], w_ref[0:MLSTM_W, :])
         + _dot(bd_ref[0], w_ref[MLSTM_W:MLSTM_W + DIFF_W, :])
         + _dot(cg_ref[0], w_ref[MLSTM_W + DIFF_W:, :]))
    z = _per_half(lambda m, xh, yh: alpha * xh + m[2:3] * yh, mods, x_ref[0], y)
    x = _layer_norm(z, g1_ref[...], b1_ref[...])
    xm = _per_half(lambda m, xh: xh * (1.0 + m[4:5]) + m[3:4], mods, x).astype(BF16)
    acc = jnp.zeros(x.shape, F32)
    for c in range(d_ff // chunk):
        gate = _dot(xm, wi_ref[:, c * chunk:(c + 1) * chunk])
        up = _dot(xm, wi_ref[:, d_ff + c * chunk:d_ff + (c + 1) * chunk])
        act = (gate * jax.nn.sigmoid(gate) * up).astype(BF16)
        acc = acc + _dot(act, wo_ref[c * chunk:(c + 1) * chunk, :])
    z = _per_half(lambda m, xh, ah: alpha * xh + m[5:6] * ah, mods, x, acc)
    o_ref[0] = _layer_norm(z, g2_ref[...], b2_ref[...])


def _post_call(a, bd, cg, xs, mod, w, g1, b1, wi, wo, g2, b2, n_lat, latent_only, alpha):
    b, t, d = xs.shape
    d_ff = wo.shape[0]
    tm = LINEAR_TILE
    full = lambda arr: _resident(arr.shape)
    if latent_only:
        s = n_lat * ROW_TILE
        assert s % tm == 0
        grid = (b, s // tm)
        row = lambda width: pl.BlockSpec((1, tm, width), lambda bi, i: (bi, i, 0))
        mod_specs = [pl.BlockSpec((1, 1, 6, d), lambda bi, i: (bi, 0, 0, 0))] * 2
        vec = pl.BlockSpec((1, d), lambda bi, i: (0, 0))
        out_shape = (b, s, d)
        sem = ("parallel", "parallel")
    else:
        flat = _FlatTiling(b, t, n_lat)
        grid = (flat.n_tiles,)
        row = lambda width: pl.BlockSpec((1, tm, width), lambda i: (0, i, 0))
        mod_specs = [flat.mod_spec(0, d), flat.mod_spec(1, d)]
        vec = pl.BlockSpec((1, d), lambda i: (0, 0))
        out_shape = (1, b * t, d)
        sem = ("parallel",)
        a, bd, cg, xs = (v.reshape(1, b * t, -1) for v in (a, bd, cg, xs))
    out = pl.pallas_call(
        functools.partial(_post_kernel, alpha=alpha, d_ff=d_ff, chunk=2 * LANES),
        grid=grid,
        in_specs=[row(MLSTM_W), row(DIFF_W), row(GQA_W), row(d), *mod_specs,
                  full(w), vec, vec, full(wi), full(wo), vec, vec],
        out_specs=row(d),
        out_shape=jax.ShapeDtypeStruct(out_shape, F32),
        compiler_params=_cparams(*sem),
        name="out_proj_ffn",
    )(a, bd, cg, xs, mod, mod, w, g1, b1, wi, wo, g2, b2)
    return out if latent_only else out.reshape(b, t, d)


def _prep_w_in(w):
    zeros = lambda n: jnp.zeros((w.shape[0], n), w.dtype)
    o_gate = 4 * MLSTM_W
    o_dq = o_gate + 4 * MLSTM_HEADS
    o_gq = o_dq + 3 * DIFF_W
    o_gk = o_gq + GQA_W
    o_gv = o_gk + GQA_KV_W
    gq = w[:, o_gq:o_gk].reshape(-1, 2, 2, HEAD_DIM).transpose(0, 2, 1, 3).reshape(-1, GQA_W)
    gv = w[:, o_gv:o_gv + GQA_KV_W]
    cols = [w[:, :o_gate], w[:, o_dq:o_gq], gq, w[:, o_gk:o_gv],
            gv[:, :HEAD_DIM], zeros(LANES), gv[:, HEAD_DIM:],
            w[:, o_gate:o_dq], zeros(LANES - 4 * MLSTM_HEADS)]
    out = jnp.concatenate(cols, axis=1)
    assert out.shape[1] == IN_COLS_PADDED
    return out.astype(BF16)


def _prep_w_out(w):
    o_c = MLSTM_W + DIFF_W
    gq = w[o_c:].reshape(2, 2, HEAD_DIM, -1).transpose(1, 0, 2, 3).reshape(GQA_W, -1)
    return jnp.concatenate([w[:o_c], gq], axis=0).astype(BF16)


def _rope_tables(s, n_ctx_rows):
    pos = jnp.arange(s, dtype=jnp.int32)
    row = (pos // GRID_W).astype(F32)
    col = (pos % GRID_W).astype(F32)
    n_freq = HEAD_DIM // 4
    inv = ROPE_THETA ** (-jnp.arange(n_freq, dtype=F32) / n_freq)
    ar = row[:, None] * inv
    ac = col[:, None] * inv
    ang = jnp.concatenate([ar, ar, ac, ac], axis=-1)
    ang = jnp.concatenate([ang, ang], axis=-1)
    first = (jnp.arange(LANES) % (2 * ROPE_HALF)) < ROPE_HALF
    cos, sin = jnp.cos(ang), jnp.sin(ang)
    sa = jnp.where(first, -sin, 0.0)
    sb = jnp.where(first, 0.0, sin)
    pad = lambda a, v: jnp.concatenate([a, jnp.full((n_ctx_rows, LANES), v, F32)], axis=0)
    return pad(cos, 1.0), pad(sa, 0.0), pad(sb, 0.0)


def kernel(x, c, ctx, c_ctx, w_ada, b_ada, w_in, mlstm_conv_w, mlstm_conv_b, mlstm_gate_b, mlstm_norm_g, diff_lambda, diff_norm_g, gqa_q_norm_g, gqa_k_norm_g, w_out, ln1_g, ln1_b, w_ffn_in, w_ffn_out, ln2_g, ln2_b):
    b, s, d = x.shape
    n_ctx_rows = ctx.shape[1]
    depth = w_in.shape[0]
    assert s % ROW_TILE == 0 and n_ctx_rows % ROW_TILE == 0 and s % GRID_W == 0
    n_lat = s // ROW_TILE
    t = s + n_ctx_rows
    alpha = (2 * depth) ** 0.25

    ada_rows = -(-(b + 1) // 8) * 8
    cvec = jnp.concatenate([c, c_ctx[None, :], jnp.zeros((ada_rows - b - 1, d), F32)], axis=0)
    mod_all = _ada_call(cvec, w_ada, b_ada)
    mod_lat = mod_all[:, :b].reshape(depth, b, 1, 6, d)
    mod_ctx = jnp.broadcast_to(mod_all[:, b].reshape(depth, 1, 1, 6, d), (depth, b, 1, 6, d))
    mod_all = jnp.concatenate([mod_lat, mod_ctx], axis=2)

    tabs = _rope_tables(s, n_ctx_rows)
    tile2 = lambda v, n: jnp.tile(v, n).reshape(1, -1)
    xs = jnp.concatenate([x, ctx], axis=1)

    for l in range(depth):
        last = l == depth - 1
        lam_init = 0.8 - 0.6 * math.exp(-0.3 * l)
        mod = mod_all[l]
        (mqk, mv, mo, gate, dq, dk, dv, gq, gk, gv) = _inproj_call(
            xs, mod, _prep_w_in(w_in[l]), tabs,
            tile2(gqa_q_norm_g[l], LANES // HEAD_DIM), tile2(gqa_k_norm_g[l], LANES // HEAD_DIM), n_lat)
        cw = jnp.concatenate([mlstm_conv_w[l], jnp.zeros((8 - CONV_W, 2 * MLSTM_W), F32)], axis=0)
        gb = jnp.concatenate([mlstm_gate_b[l], jnp.zeros((LANES - 4 * MLSTM_HEADS,), F32)]).reshape(1, LANES)
        a = _mlstm_call(mqk, mv, mo, gate, cw, mlstm_conv_b[l].reshape(1, -1), gb,
                        mlstm_norm_g[l].reshape(1, -1), n_lat)
        bd = _diff_attn_call(dq, dk, dv, diff_lambda[l][None], diff_norm_g[l].reshape(1, -1), n_lat, lam_init)
        cg = _gqa_attn_call(gq, gk, gv, n_lat)
        xs = _post_call(a, bd, cg, xs, mod, _prep_w_out(w_out[l]),
                        ln1_g[l].reshape(1, -1), ln1_b[l].reshape(1, -1),
                        w_ffn_in[l].astype(BF16), w_ffn_out[l].astype(BF16),
                        ln2_g[l].reshape(1, -1), ln2_b[l].reshape(1, -1), n_lat, last, alpha)
    return xs
```

```python
import functools
import math

import jax
import jax.numpy as jnp
from jax import lax
from jax.experimental import pallas as pl
from jax.experimental.pallas import tpu as pltpu

F32 = jnp.float32
BF16 = jnp.bfloat16

HEAD_DIM = 64
MLSTM_HEADS = 4
DIFF_HEADS = 4
GQA_HEADS = 4
GQA_KV_HEADS = 2
GRID_W = 64
CONV_W = 5
ROPE_THETA = 10000.0
LN_EPS = 1e-5

MLSTM_W = MLSTM_HEADS * HEAD_DIM
DIFF_W = DIFF_HEADS * 2 * HEAD_DIM
GQA_W = GQA_HEADS * HEAD_DIM
GQA_KV_W = GQA_KV_HEADS * HEAD_DIM

LANES = 128
BF16_SUBLANES = 16
VMEM_LIMIT_BYTES = 56 * 1024 * 1024

ROW_TILE = 256
LINEAR_TILE = 512
ROPE_HALF = HEAD_DIM // 4
DIFF_TILES = (512, 1024, 2)
GQA_TILES = (256, 4096, 4)
ATTN_Q_SCALE = HEAD_DIM ** -0.5 * math.log2(math.e)

C_MQK = 0
C_MV = C_MQK + 2 * MLSTM_W
C_MO = C_MV + MLSTM_W
C_DQ = C_MO + MLSTM_W
C_DK = C_DQ + DIFF_W
C_DV = C_DK + DIFF_W
C_GQ = C_DV + DIFF_W
C_GK = C_GQ + GQA_W
C_GV = C_GK + GQA_KV_W
C_GATE = C_GV + 2 * LANES
IN_COLS_PADDED = C_GATE + LANES


def _cparams(*sem):
    return pltpu.CompilerParams(dimension_semantics=sem, vmem_limit_bytes=VMEM_LIMIT_BYTES)


def _resident(shape):
    return pl.BlockSpec(shape, lambda *_: (0,) * len(shape), pipeline_mode=pl.Buffered(1))


class _FlatTiling:
    def __init__(self, b, t, n_lat):
        self.tm = LINEAR_TILE
        assert LINEAR_TILE == 2 * ROW_TILE and (b * t) % self.tm == 0
        self.n_tiles = b * t // self.tm
        self.n_tot = t // ROW_TILE
        self.n_lat = n_lat

    def unit_in_batch(self, i, h):
        return (2 * i + h) % self.n_tot

    def mod_spec(self, h, d):
        return pl.BlockSpec(
            (1, 1, 6, d),
            lambda i: ((2 * i + h) // self.n_tot, jnp.minimum(self.unit_in_batch(i, h) // self.n_lat, 1), 0, 0))


def _dot(a, b):
    return jnp.dot(a, b, preferred_element_type=F32)


def _dot_nt(a, b):
    return lax.dot_general(a, b, (((1,), (1,)), ((), ())), preferred_element_type=F32)


def _split_bf16(a):
    hi = a.astype(BF16)
    lo = (a - hi.astype(F32)).astype(BF16)
    return hi, lo


def _dot_split_lhs(a, m):
    hi, lo = _split_bf16(a)
    return _dot(hi, m) + _dot(lo, m)


def _dot_split_rhs(m, a):
    hi, lo = _split_bf16(a)
    return _dot(m, hi) + _dot(m, lo)


def _head_mean_matrix(width):
    r = lax.broadcasted_iota(jnp.int32, (width, width), 0) // HEAD_DIM
    c = lax.broadcasted_iota(jnp.int32, (width, width), 1) // HEAD_DIM
    return jnp.where(r == c, 1.0 / HEAD_DIM, 0.0).astype(BF16)


def _layer_norm(z, g, b):
    mu = jnp.mean(z, axis=-1, keepdims=True)
    zc = z - mu
    var = jnp.mean(zc * zc, axis=-1, keepdims=True)
    return zc * lax.rsqrt(var + LN_EPS) * g + b


def _ada_kernel(c_ref, w_ref, b_ref, o_ref):
    c = c_ref[...]
    o_ref[0] = _dot(c * jax.nn.sigmoid(c), w_ref[0]) + b_ref[0]


def _ada_call(cvec, w_ada, b_ada):
    depth, d, six_d = w_ada.shape
    rows = cvec.shape[0]
    bn = 1024
    return pl.pallas_call(
        _ada_kernel,
        grid=(depth, six_d // bn),
        in_specs=[pl.BlockSpec((rows, d), lambda l, j: (0, 0)),
                  pl.BlockSpec((1, d, bn), lambda l, j: (l, 0, j)),
                  pl.BlockSpec((1, 1, bn), lambda l, j: (l, 0, j))],
        out_specs=pl.BlockSpec((1, rows, bn), lambda l, j: (l, 0, j)),
        out_shape=jax.ShapeDtypeStruct((depth, rows, six_d), F32),
        compiler_params=_cparams("parallel", "parallel"),
        name="adaln_mod",
    )(cvec, w_ada, b_ada.reshape(depth, 1, six_d))


def _per_half(fn, mods, *arrays):
    half = arrays[0].shape[0] // 2
    return jnp.concatenate([fn(m, *(a[i * half:(i + 1) * half] for a in arrays))
                            for i, m in enumerate(mods)], axis=0)


def _inproj_kernel(x_ref, moda_ref, modb_ref, w_ref, cosa_ref, cosb_ref, saa_ref, sab_ref,
                   sba_ref, sbb_ref, qg_ref, kg_ref,
                   mqk_ref, mv_ref, mo_ref, gate_ref, dq_ref, dk_ref, dv_ref,
                   gq_ref, gk_ref, gv_ref):
    mods = (moda_ref[0, 0], modb_ref[0, 0])
    xm = _per_half(lambda m, x: x * (1.0 + m[1:2]) + m[0:1], mods, x_ref[0]).astype(BF16)
    cos = jnp.concatenate([cosa_ref[...], cosb_ref[...]], axis=0)
    sa = jnp.concatenate([saa_ref[...], sab_ref[...]], axis=0)
    sb = jnp.concatenate([sba_ref[...], sbb_ref[...]], axis=0)
    scale = ATTN_Q_SCALE

    def proj(lo, n):
        return _dot(xm, w_ref[:, lo:lo + n])

    def rope(a):
        return (a * cos + pltpu.roll(a, LANES - ROPE_HALF, 1) * sa
                + pltpu.roll(a, ROPE_HALF, 1) * sb)

    mqk_ref[0] = proj(C_MQK, 2 * MLSTM_W).astype(BF16)
    mv_ref[0] = proj(C_MV, MLSTM_W).astype(BF16)
    mo_ref[0] = proj(C_MO, MLSTM_W).astype(BF16)
    gate_ref[0] = proj(C_GATE, LANES)

    acc = proj(C_DQ, DIFF_W)
    for j in range(DIFF_W // LANES):
        dq_ref[0, :, j * LANES:(j + 1) * LANES] = (
            rope(acc[:, j * LANES:(j + 1) * LANES]) * scale).astype(BF16)
    acc = proj(C_DK, DIFF_W)
    for j in range(DIFF_W // LANES):
        dk_ref[0, :, j * LANES:(j + 1) * LANES] = rope(acc[:, j * LANES:(j + 1) * LANES]).astype(BF16)
    dv_ref[0] = proj(C_DV, DIFF_W).astype(BF16)

    head_mean = _head_mean_matrix(LANES)

    def rms(a, g):
        ms = _dot_split_lhs(a * a, head_mean)
        return a * lax.rsqrt(ms + LN_EPS) * g

    acc = proj(C_GQ, GQA_W)
    qg = qg_ref[...]
    for j in range(GQA_W // LANES):
        gq_ref[0, :, j * LANES:(j + 1) * LANES] = (
            rope(rms(acc[:, j * LANES:(j + 1) * LANES], qg)) * scale).astype(BF16)
    gk_ref[0] = rope(rms(proj(C_GK, GQA_KV_W), kg_ref[...])).astype(BF16)

    lane = lax.broadcasted_iota(jnp.int32, (1, 2 * LANES), 1)
    ones_col = jnp.logical_or(lane == HEAD_DIM, lane == LANES).astype(F32)
    gv_ref[0] = (proj(C_GV, 2 * LANES) + ones_col).astype(BF16)


def _inproj_call(xs, mod, w, tabs, qg, kg, n_lat):
    b, t, d = xs.shape
    flat = _FlatTiling(b, t, n_lat)
    row = lambda width: pl.BlockSpec((1, flat.tm, width), lambda i: (0, i, 0))
    tab = lambda h: pl.BlockSpec((ROW_TILE, LANES), lambda i: (flat.unit_in_batch(i, h), 0))
    vec = pl.BlockSpec((1, LANES), lambda i: (0, 0))
    widths = (2 * MLSTM_W, MLSTM_W, MLSTM_W, LANES, DIFF_W, DIFF_W, DIFF_W, GQA_W, GQA_KV_W, 2 * LANES)
    dtypes = (BF16, BF16, BF16, F32, BF16, BF16, BF16, BF16, BF16, BF16)
    outs = pl.pallas_call(
        _inproj_kernel,
        grid=(flat.n_tiles,),
        in_specs=[row(d), flat.mod_spec(0, d), flat.mod_spec(1, d),
                  _resident((d, IN_COLS_PADDED)),
                  tab(0), tab(1), tab(0), tab(1), tab(0), tab(1), vec, vec],
        out_specs=[row(wd) for wd in widths],
        out_shape=[jax.ShapeDtypeStruct((1, b * t, wd), dt) for wd, dt in zip(widths, dtypes)],
        compiler_params=_cparams("parallel"),
        name="in_proj",
    )(xs.reshape(1, b * t, d), mod, mod, w, tabs[0], tabs[0], tabs[1], tabs[1], tabs[2], tabs[2], qg, kg)
    return [o.reshape(b, t, -1) for o in outs]


def _mlstm_kernel(mqk_ref, mv_ref, mo_ref, gate_ref, cw_ref, cb_ref, gb_ref, ng_ref, out_ref,
                  qk_s, h_s, *, n_lat, n_tot):
    lc = ROW_TILE
    t_rows = n_tot * lc
    halo = BF16_SUBLANES

    cw = cw_ref[...]
    cb = cb_ref[...]
    lane_qk = lax.broadcasted_iota(jnp.int32, (1, 2 * MLSTM_W), 1)
    qscale = jnp.where(lane_qk < MLSTM_W, HEAD_DIM ** -0.5, 1.0)

    def conv_body(c, carry):
        r0 = pl.multiple_of(c * lc, lc)
        main = mqk_ref[0, pl.ds(r0, lc), :].astype(F32)
        ts = pl.multiple_of(jnp.maximum(r0 - halo, 0), halo)
        bs = pl.multiple_of(jnp.minimum(r0 + lc, t_rows - halo), halo)
        top_ok = jnp.logical_and(c != 0, c != n_lat).astype(F32)
        bot_ok = jnp.logical_and(c != n_lat - 1, c != n_tot - 1).astype(F32)
        top = mqk_ref[0, pl.ds(ts, halo), :].astype(F32) * top_ok
        bot = mqk_ref[0, pl.ds(bs, halo), :].astype(F32) * bot_ok
        win = jnp.concatenate([top, main, bot], axis=0)
        n = lc + 2 * halo
        y = win * cw[CONV_W // 2:CONV_W // 2 + 1]
        for j in range(CONV_W):
            if j != CONV_W // 2:
                y = y + pltpu.roll(win, (CONV_W // 2 - j) % n, 0) * cw[j:j + 1]
        y = y[halo:halo + lc] + cb
        qk_s[pl.ds(r0, lc), :] = (y * jax.nn.sigmoid(y) * qscale).astype(BF16)
        return carry

    lax.fori_loop(0, n_tot, conv_body, 0)

    row_i = lax.broadcasted_iota(jnp.int32, (lc, lc), 0)
    col_i = lax.broadcasted_iota(jnp.int32, (lc, lc), 1)
    masks_t = (row_i <= col_i, row_i >= col_i)
    tri_row = (masks_t[0].astype(BF16), masks_t[1].astype(BF16))
    tri_col = (tri_row[1], tri_row[0])
    sub = lax.broadcasted_iota(jnp.int32, (LANES, lc), 0)
    sels = (sub < HEAD_DIM, sub >= HEAD_DIM)
    one_rows = ((sub == HEAD_DIM).astype(BF16), (sub == 0).astype(BF16))
    den_row = (HEAD_DIM, 0)
    gb = gb_ref[...]
    h_s[...] = jnp.zeros_like(h_s)
    n_streams = 2 * MLSTM_HEADS

    def step(j, carry):
        cns, ms = carry
        new_cns, new_ms = list(cns), list(ms)
        for d in range(2):
            c = (j + n_lat) % n_tot if d == 0 else n_tot - 1 - j
            r0 = pl.multiple_of(c * lc, lc)
            gates = gate_ref[0, pl.ds(r0, lc), :] + gb
            gates_t = gates.T
            cum = _dot_split_rhs(tri_col[d], jax.nn.log_sigmoid(gates))
            cum_t = _dot_split_lhs(jax.nn.log_sigmoid(gates_t), tri_row[d])
            qk = qk_s[pl.ds(r0, lc), :]
            vv = mv_ref[0, pl.ds(r0, lc), :]
            last = lc - 1 if d == 0 else 0
            for p in range(MLSTM_HEADS // 2):
                q_t = qk[:, p * LANES:(p + 1) * LANES].T
                k128 = qk[:, MLSTM_W + p * LANES:MLSTM_W + (p + 1) * LANES]
                v_t = vv[:, p * LANES:(p + 1) * LANES].T
                halves = []
                for e in range(2):
                    hh = 2 * p + e
                    sidx = d * MLSTM_HEADS + hh
                    ci = d * MLSTM_HEADS + hh
                    cf = 2 * MLSTM_HEADS + d * MLSTM_HEADS + hh
                    qm_t = jnp.where(sels[e], q_t, jnp.zeros_like(q_t))
                    v1_t = jnp.where(sels[e], v_t, one_rows[e])
                    i_row = gates_t[ci:ci + 1, :]
                    a_row = cum_t[cf:cf + 1, :]
                    r_col = gates[:, ci:ci + 1] - cum[:, cf:cf + 1]
                    m = ms[sidx]
                    cn_t = cns[sidx]
                    dm = jnp.where(masks_t[d], a_row + r_col, -jnp.inf)
                    inter = a_row + m
                    m_t = jnp.maximum(jnp.max(dm, axis=0, keepdims=True), inter)
                    w_prev = jnp.exp(inter - m_t)
                    sw = (_dot(k128, qm_t) * jnp.exp(dm - m_t)).astype(BF16)
                    nd = _dot(v1_t, sw) + w_prev * _dot(cn_t.astype(BF16), qm_t)
                    den = nd[den_row[e]:den_row[e] + 1, :]
                    halves.append(nd * (1.0 / jnp.maximum(jnp.abs(den), jnp.exp(-m_t))))
                    b_last = cum_t[cf:cf + 1, last:last + 1]
                    logu = b_last - a_row + i_row
                    m_new = jnp.maximum(b_last + m, jnp.max(logu, axis=1, keepdims=True))
                    vu = (v1_t.astype(F32) * jnp.exp(logu - m_new)).astype(BF16)
                    new_cns[sidx] = jnp.exp(b_last + m - m_new) * cn_t + _dot(vu, k128)
                    new_ms[sidx] = m_new
                h_pair = jnp.where(sels[0], halves[0], halves[1]).T
                h_s[pl.ds(r0, lc), p * LANES:(p + 1) * LANES] += h_pair
        return tuple(new_cns), tuple(new_ms)

    init = (tuple(jnp.zeros((LANES, LANES), F32) for _ in range(n_streams)),
            tuple(jnp.zeros((1, 1), F32) for _ in range(n_streams)))
    lax.fori_loop(0, n_tot, step, init, unroll=2)

    head_mean = _head_mean_matrix(MLSTM_W)
    ng = ng_ref[...]

    def out_body(c, carry):
        r0 = pl.multiple_of(c * lc, lc)
        hb = h_s[pl.ds(r0, lc), :]
        hc = hb - _dot_split_lhs(hb, head_mean)
        var = _dot_split_lhs(hc * hc, head_mean)
        o = mo_ref[0, pl.ds(r0, lc), :].astype(F32)
        out_ref[0, pl.ds(r0, lc), :] = (hc * lax.rsqrt(var + LN_EPS) * ng * jax.nn.sigmoid(o)).astype(BF16)
        return carry

    lax.fori_loop(0, n_tot, out_body, 0, unroll=2)


def _mlstm_call(mqk, mv, mo, gate, cw, cb, gb, ng, n_lat):
    b, t, _ = mqk.shape
    n_tot = t // ROW_TILE
    seq = lambda width: pl.BlockSpec((1, t, width), lambda bi: (bi, 0, 0))
    full = lambda a: pl.BlockSpec(a.shape, lambda bi: (0, 0))
    return pl.pallas_call(
        functools.partial(_mlstm_kernel, n_lat=n_lat, n_tot=n_tot),
        grid=(b,),
        in_specs=[seq(2 * MLSTM_W), seq(MLSTM_W), seq(MLSTM_W), seq(LANES),
                  full(cw), full(cb), full(gb), full(ng)],
        out_specs=seq(MLSTM_W),
        out_shape=jax.ShapeDtypeStruct((b, t, MLSTM_W), BF16),
        scratch_shapes=[pltpu.VMEM((t, 2 * MLSTM_W), BF16), pltpu.VMEM((t, MLSTM_W), F32)],
        compiler_params=_cparams("parallel"),
        name="mlstm",
    )(mqk, mv, mo, gate, cw, cb, gb, ng)


def _key_blocks(first, n_rows, tk):
    blocks, r = [], first
    while r < first + n_rows:
        size = min(tk, first + n_rows - r)
        blocks.append((r, size))
        r += size
    return blocks


def _attn_pair_tile(qms, k_ref, v1_fn, blocks):
    ms, accs = [None, None], [None, None]
    for start, size in blocks:
        k = k_ref[0, pl.ds(start, size), :]
        for e in range(2):
            s = _dot_nt(qms[e], k)
            row_max = jnp.max(s, axis=1, keepdims=True)
            m_new = row_max if ms[e] is None else jnp.maximum(ms[e], row_max)
            pv = _dot(jnp.exp2(s - m_new).astype(BF16), v1_fn(e, start, size))
            accs[e] = pv if accs[e] is None else jnp.exp2(ms[e] - m_new) * accs[e] + pv
            ms[e] = m_new
    return accs


def _for_query_tiles(n_lat_rows, n_ctx_rows, tiles, tile_fn):
    tq, tk, unroll = tiles
    tq = min(tq, n_lat_rows)
    lat_blocks = _key_blocks(0, n_lat_rows, tk) + _key_blocks(n_lat_rows, n_ctx_rows, tk)
    ctx_blocks = _key_blocks(n_lat_rows, n_ctx_rows, tk)

    def lat_body(i, carry):
        tile_fn(pl.multiple_of(i * tq, tq), tq, lat_blocks)
        return carry

    lax.fori_loop(0, n_lat_rows // tq, lat_body, 0, unroll=unroll)
    tq_ctx = min(tq, n_ctx_rows)
    for i in range(n_ctx_rows // tq_ctx):
        tile_fn(n_lat_rows + i * tq_ctx, tq_ctx, ctx_blocks)


def _diff_attn_kernel(q_ref, k_ref, v_ref, lam_ref, g_ref, o_ref, *, n_lat_rows, tiles, lam_init):
    dv = 2 * HEAD_DIM
    n_ctx_rows = q_ref.shape[1] - n_lat_rows
    lane = lax.broadcasted_iota(jnp.int32, (1, LANES), 1)
    ones_blk = jnp.broadcast_to((lane == 0).astype(BF16), (tiles[1], LANES))
    lv = lam_ref[0]
    lam = (jnp.exp(jnp.sum(lv[0:1] * lv[1:2], axis=1, keepdims=True))
           - jnp.exp(jnp.sum(lv[2:3] * lv[3:4], axis=1, keepdims=True)) + lam_init)
    gain = g_ref[...] * (1.0 - lam_init)

    def v1_fn(e, start, size):
        return jnp.concatenate([v_ref[0, pl.ds(start, size), :], ones_blk[:size]], axis=1)

    def tile_fn(r0, rows, blocks):
        q = q_ref[0, pl.ds(r0, rows), :]
        zero = jnp.zeros_like(q)
        qms = (jnp.where(lane < HEAD_DIM, q, zero), jnp.where(lane >= HEAD_DIM, q, zero))
        a1, a2 = _attn_pair_tile(qms, k_ref, v1_fn, blocks)
        o = a1[:, :dv] * (1.0 / a1[:, dv:dv + 1]) - lam * (a2[:, :dv] * (1.0 / a2[:, dv:dv + 1]))
        ms = jnp.mean(o * o, axis=1, keepdims=True)
        o_ref[0, pl.ds(r0, rows), :] = (o * lax.rsqrt(ms + LN_EPS) * gain).astype(BF16)

    _for_query_tiles(n_lat_rows, n_ctx_rows, tiles, tile_fn)


def _diff_attn_call(dq, dk, dv, lam_vecs, g, n_lat, lam_init):
    b, t, _ = dq.shape
    seq = pl.BlockSpec((1, t, LANES), lambda bi, h: (bi, 0, h))
    return pl.pallas_call(
        functools.partial(_diff_attn_kernel, n_lat_rows=n_lat * ROW_TILE, tiles=DIFF_TILES,
                          lam_init=lam_init),
        grid=(b, DIFF_HEADS),
        in_specs=[seq, seq, seq,
                  pl.BlockSpec((1, 4, HEAD_DIM), lambda bi, h: (0, 0, 0)),
                  pl.BlockSpec((1, LANES), lambda bi, h: (0, 0))],
        out_specs=seq,
        out_shape=jax.ShapeDtypeStruct((b, t, DIFF_W), BF16),
        compiler_params=_cparams("parallel", "parallel"),
        name="diff_attn",
    )(dq, dk, dv, lam_vecs, g)


def _gqa_attn_kernel(q_ref, k_ref, v_ref, o_ref, *, n_lat_rows, tiles):
    n_ctx_rows = q_ref.shape[1] - n_lat_rows
    lane = lax.broadcasted_iota(jnp.int32, (1, LANES), 1)
    first_half = lane < HEAD_DIM
    den_lane = (HEAD_DIM, 0)

    def v1_fn(e, start, size):
        return v_ref[0, pl.ds(start, size), e * LANES:(e + 1) * LANES]

    def tile_fn(r0, rows, blocks):
        q = q_ref[0, pl.ds(r0, rows), :]
        zero = jnp.zeros_like(q)
        qms = (jnp.where(first_half, q, zero), jnp.where(first_half, zero, q))
        accs = _attn_pair_tile(qms, k_ref, v1_fn, blocks)
        halves = [acc * (1.0 / acc[:, den_lane[e]:den_lane[e] + 1]) for e, acc in enumerate(accs)]
        o_ref[0, pl.ds(r0, rows), :] = jnp.where(first_half, halves[0], halves[1]).astype(BF16)

    _for_query_tiles(n_lat_rows, n_ctx_rows, tiles, tile_fn)


def _gqa_attn_call(gq, gk, gv, n_lat):
    b, t, _ = gq.shape
    pair = pl.BlockSpec((1, t, LANES), lambda bi, p: (bi, 0, p))
    return pl.pallas_call(
        functools.partial(_gqa_attn_kernel, n_lat_rows=n_lat * ROW_TILE, tiles=GQA_TILES),
        grid=(b, GQA_W // LANES),
        in_specs=[pair,
                  pl.BlockSpec((1, t, GQA_KV_W), lambda bi, p: (bi, 0, 0)),
                  pl.BlockSpec((1, t, 2 * LANES), lambda bi, p: (bi, 0, 0))],
        out_specs=pair,
        out_shape=jax.ShapeDtypeStruct((b, t, GQA_W), BF16),
        compiler_params=_cparams("parallel", "arbitrary"),
        name="gqa_attn",
    )(gq, gk, gv)


def _post_kernel(a_ref, bd_ref, cg_ref, x_ref, moda_ref, modb_ref, w_ref, g1_ref, b1_ref,
                 wi_ref, wo_ref, g2_ref, b2_ref, o_ref, *, alpha, d_ff, chunk):
    mods = (moda_ref[0, 0], modb_ref[0, 0])
    y = (_dot(a_ref[0], w_ref[0:MLSTM_W, :])
         + _dot(bd_ref[0], w_ref[MLSTM_W:MLSTM_W + DIFF_W, :])
         + _dot(cg_ref[0], w_ref[MLSTM_W + DIFF_W:, :]))
    z = _per_half(lambda m, xh, yh: alpha * xh + m[2:3] * yh, mods, x_ref[0], y)
    x = _layer_norm(z, g1_ref[...], b1_ref[...])
    xm = _per_half(lambda m, xh: xh * (1.0 + m[4:5]) + m[3:4], mods, x).astype(BF16)
    acc = jnp.zeros(x.shape, F32)
    for c in range(d_ff // chunk):
        gate = _dot(xm, wi_ref[:, c * chunk:(c + 1) * chunk])
        up = _dot(xm, wi_ref[:, d_ff + c * chunk:d_ff + (c + 1) * chunk])
        act = (gate * jax.nn.sigmoid(gate) * up).astype(BF16)
        acc = acc + _dot(act, wo_ref[c * chunk:(c + 1) * chunk, :])
    z = _per_half(lambda m, xh, ah: alpha * xh + m[5:6] * ah, mods, x, acc)
    o_ref[0] = _layer_norm(z, g2_ref[...], b2_ref[...])


def _post_call(a, bd, cg, xs, mod, w, g1, b1, wi, wo, g2, b2, n_lat, latent_only, alpha):
    b, t, d = xs.shape
    d_ff = wo.shape[0]
    tm = LINEAR_TILE
    full = lambda arr: _resident(arr.shape)
    if latent_only:
        s = n_lat * ROW_TILE
        assert s % tm == 0
        grid = (b, s // tm)
        row = lambda width: pl.BlockSpec((1, tm, width), lambda bi, i: (bi, i, 0))
        mod_specs = [pl.BlockSpec((1, 1, 6, d), lambda bi, i: (bi, 0, 0, 0))] * 2
        vec = pl.BlockSpec((1, d), lambda bi, i: (0, 0))
        out_shape = (b, s, d)
        sem = ("parallel", "parallel")
    else:
        flat = _FlatTiling(b, t, n_lat)
        grid = (flat.n_tiles,)
        row = lambda width: pl.BlockSpec((1, tm, width), lambda i: (0, i, 0))
        mod_specs = [flat.mod_spec(0, d), flat.mod_spec(1, d)]
        vec = pl.BlockSpec((1, d), lambda i: (0, 0))
        out_shape = (1, b * t, d)
        sem = ("parallel",)
        a, bd, cg, xs = (v.reshape(1, b * t, -1) for v in (a, bd, cg, xs))
    out = pl.pallas_call(
        functools.partial(_post_kernel, alpha=alpha, d_ff=d_ff, chunk=2 * LANES),
        grid=grid,
        in_specs=[row(MLSTM_W), row(DIFF_W), row(GQA_W), row(d), *mod_specs,
                  full(w), vec, vec, full(wi), full(wo), vec, vec],
        out_specs=row(d),
        out_shape=jax.ShapeDtypeStruct(out_shape, F32),
        compiler_params=_cparams(*sem),
        name="out_proj_ffn",
    )(a, bd, cg, xs, mod, mod, w, g1, b1, wi, wo, g2, b2)
    return out if latent_only else out.reshape(b, t, d)


def _prep_w_in(w):
    zeros = lambda n: jnp.zeros((w.shape[0], n), w.dtype)
    o_gate = 4 * MLSTM_W
    o_dq = o_gate + 4 * MLSTM_HEADS
    o_gq = o_dq + 3 * DIFF_W
    o_gk = o_gq + GQA_W
    o_gv = o_gk + GQA_KV_W
    gq = w[:, o_gq:o_gk].reshape(-1, 2, 2, HEAD_DIM).transpose(0, 2, 1, 3).reshape(-1, GQA_W)
    gv = w[:, o_gv:o_gv + GQA_KV_W]
    cols = [w[:, :o_gate], w[:, o_dq:o_gq], gq, w[:, o_gk:o_gv],
            gv[:, :HEAD_DIM], zeros(LANES), gv[:, HEAD_DIM:],
            w[:, o_gate:o_dq], zeros(LANES - 4 * MLSTM_HEADS)]
    out = jnp.concatenate(cols, axis=1)
    assert out.shape[1] == IN_COLS_PADDED
    return out.astype(BF16)


def _prep_w_out(w):
    o_c = MLSTM_W + DIFF_W
    gq = w[o_c:].reshape(2, 2, HEAD_DIM, -1).transpose(1, 0, 2, 3).reshape(GQA_W, -1)
    return jnp.concatenate([w[:o_c], gq], axis=0).astype(BF16)


def _rope_tables(s, n_ctx_rows):
    pos = jnp.arange(s, dtype=jnp.int32)
    row = (pos // GRID_W).astype(F32)
    col = (pos % GRID_W).astype(F32)
    n_freq = HEAD_DIM // 4
    inv = ROPE_THETA ** (-jnp.arange(n_freq, dtype=F32) / n_freq)
    ar = row[:, None] * inv
    ac = col[:, None] * inv
    ang = jnp.concatenate([ar, ar, ac, ac], axis=-1)
    ang = jnp.concatenate([ang, ang], axis=-1)
    first = (jnp.arange(LANES) % (2 * ROPE_HALF)) < ROPE_HALF
    cos, sin = jnp.cos(ang), jnp.sin(ang)
    sa = jnp.where(first, -sin, 0.0)
    sb = jnp.where(first, 0.0, sin)
    pad = lambda a, v: jnp.concatenate([a, jnp.full((n_ctx_rows, LANES), v, F32)], axis=0)
    return pad(cos, 1.0), pad(sa, 0.0), pad(sb, 0.0)


def kernel(x, c, ctx, c_ctx, w_ada, b_ada, w_in, mlstm_conv_w, mlstm_conv_b, mlstm_gate_b, mlstm_norm_g, diff_lambda, diff_norm_g, gqa_q_norm_g, gqa_k_norm_g, w_out, ln1_g, ln1_b, w_ffn_in, w_ffn_out, ln2_g, ln2_b):
    b, s, d = x.shape
    n_ctx_rows = ctx.shape[1]
    depth = w_in.shape[0]
    assert s % ROW_TILE == 0 and n_ctx_rows % ROW_TILE == 0 and s % GRID_W == 0
    n_lat = s // ROW_TILE
    t = s + n_ctx_rows
    alpha = (2 * depth) ** 0.25

    ada_rows = -(-(b + 1) // 8) * 8
    cvec = jnp.concatenate([c, c_ctx[None, :], jnp.zeros((ada_rows - b - 1, d), F32)], axis=0)
    mod_all = _ada_call(cvec, w_ada, b_ada)
    mod_lat = mod_all[:, :b].reshape(depth, b, 1, 6, d)
    mod_ctx = jnp.broadcast_to(mod_all[:, b].reshape(depth, 1, 1, 6, d), (depth, b, 1, 6, d))
    mod_all = jnp.concatenate([mod_lat, mod_ctx], axis=2)

    tabs = _rope_tables(s, n_ctx_rows)
    tile2 = lambda v, n: jnp.tile(v, n).reshape(1, -1)
    xs = jnp.concatenate([x, ctx], axis=1)

    for l in range(depth):
        last = l == depth - 1
        lam_init = 0.8 - 0.6 * math.exp(-0.3 * l)
        mod = mod_all[l]
        (mqk, mv, mo, gate, dq, dk, dv, gq, gk, gv) = _inproj_call(
            xs, mod, _prep_w_in(w_in[l]), tabs,
            tile2(gqa_q_norm_g[l], LANES // HEAD_DIM), tile2(gqa_k_norm_g[l], LANES // HEAD_DIM), n_lat)
        cw = jnp.concatenate([mlstm_conv_w[l], jnp.zeros((8 - CONV_W, 2 * MLSTM_W), F32)], axis=0)
        gb = jnp.concatenate([mlstm_gate_b[l], jnp.zeros((LANES - 4 * MLSTM_HEADS,), F32)]).reshape(1, LANES)
        a = _mlstm_call(mqk, mv, mo, gate, cw, mlstm_conv_b[l].reshape(1, -1), gb,
                        mlstm_norm_g[l].reshape(1, -1), n_lat)
        bd = _diff_attn_call(dq, dk, dv, diff_lambda[l][None], diff_norm_g[l].reshape(1, -1), n_lat, lam_init)
        cg = _gqa_attn_call(gq, gk, gv, n_lat)
        xs = _post_call(a, bd, cg, xs, mod, _prep_w_out(w_out[l]),
                        ln1_g[l].reshape(1, -1), ln1_b[l].reshape(1, -1),
                        w_ffn_in[l].astype(BF16), w_ffn_out[l].astype(BF16),
                        ln2_g[l].reshape(1, -1), ln2_b[l].reshape(1, -1), n_lat, last, alpha)
    return xs
```

```python
import functools
import math

import jax
import jax.numpy as jnp
from jax import lax
from jax.experimental import pallas as pl
from jax.experimental.pallas import tpu as pltpu

F32 = jnp.float32
BF16 = jnp.bfloat16

HEAD_DIM = 64
MLSTM_HEADS = 4
DIFF_HEADS = 4
GQA_HEADS = 4
GQA_KV_HEADS = 2
GRID_W = 64
CONV_W = 5
ROPE_THETA = 10000.0
LN_EPS = 1e-5

MLSTM_W = MLSTM_HEADS * HEAD_DIM
DIFF_W = DIFF_HEADS * 2 * HEAD_DIM
GQA_W = GQA_HEADS * HEAD_DIM
GQA_KV_W = GQA_KV_HEADS * HEAD_DIM

LANES = 128
BF16_SUBLANES = 16
VMEM_LIMIT_BYTES = 56 * 1024 * 1024

ROW_TILE = 256
LINEAR_TILE = 512
ROPE_HALF = HEAD_DIM // 4
DIFF_TILES = (512, 1024, 2, False)
GQA_TILES = (256, 4096, 4, True)
ATTN_Q_SCALE = HEAD_DIM ** -0.5 * math.log2(math.e)

C_MQK = 0
C_MV = C_MQK + 2 * MLSTM_W
C_MO = C_MV + MLSTM_W
C_DQ = C_MO + MLSTM_W
C_DK = C_DQ + DIFF_W
C_DV = C_DK + DIFF_W
C_GQ = C_DV + DIFF_W
C_GK = C_GQ + GQA_W
C_GV = C_GK + GQA_KV_W
C_GATE = C_GV + 2 * LANES
IN_COLS_PADDED = C_GATE + LANES


def _cparams(*sem):
    return pltpu.CompilerParams(dimension_semantics=sem, vmem_limit_bytes=VMEM_LIMIT_BYTES)


def _resident(shape):
    return pl.BlockSpec(shape, lambda *_: (0,) * len(shape), pipeline_mode=pl.Buffered(1))


class _FlatTiling:
    def __init__(self, b, t, n_lat):
        self.tm = LINEAR_TILE
        assert LINEAR_TILE == 2 * ROW_TILE and (b * t) % self.tm == 0
        self.n_tiles = b * t // self.tm
        self.n_tot = t // ROW_TILE
        self.n_lat = n_lat

    def unit_in_batch(self, i, h):
        return (2 * i + h) % self.n_tot

    def mod_spec(self, h, d):
        return pl.BlockSpec(
            (1, 1, 6, d),
            lambda i: ((2 * i + h) // self.n_tot, jnp.minimum(self.unit_in_batch(i, h) // self.n_lat, 1), 0, 0))


def _dot(a, b):
    return jnp.dot(a, b, preferred_element_type=F32)


def _dot_nt(a, b):
    return lax.dot_general(a, b, (((1,), (1,)), ((), ())), preferred_element_type=F32)


def _split_bf16(a):
    hi = a.astype(BF16)
    lo = (a - hi.astype(F32)).astype(BF16)
    return hi, lo


def _dot_split_lhs(a, m):
    hi, lo = _split_bf16(a)
    return _dot(hi, m) + _dot(lo, m)


def _dot_split_rhs(m, a):
    hi, lo = _split_bf16(a)
    return _dot(m, hi) + _dot(m, lo)


def _head_mean_matrix(width):
    r = lax.broadcasted_iota(jnp.int32, (width, width), 0) // HEAD_DIM
    c = lax.broadcasted_iota(jnp.int32, (width, width), 1) // HEAD_DIM
    return jnp.where(r == c, 1.0 / HEAD_DIM, 0.0).astype(BF16)


def _layer_norm(z, g, b):
    mu = jnp.mean(z, axis=-1, keepdims=True)
    zc = z - mu
    var = jnp.mean(zc * zc, axis=-1, keepdims=True)
    return zc * lax.rsqrt(var + LN_EPS) * g + b


def _ada_kernel(c_ref, w_ref, b_ref, o_ref):
    c = c_ref[...]
    o_ref[0] = _dot(c * jax.nn.sigmoid(c), w_ref[0]) + b_ref[0]


def _ada_call(cvec, w_ada, b_ada):
    depth, d, six_d = w_ada.shape
    rows = cvec.shape[0]
    bn = 1024
    return pl.pallas_call(
        _ada_kernel,
        grid=(depth, six_d // bn),
        in_specs=[pl.BlockSpec((rows, d), lambda l, j: (0, 0)),
                  pl.BlockSpec((1, d, bn), lambda l, j: (l, 0, j)),
                  pl.BlockSpec((1, 1, bn), lambda l, j: (l, 0, j))],
        out_specs=pl.BlockSpec((1, rows, bn), lambda l, j: (l, 0, j)),
        out_shape=jax.ShapeDtypeStruct((depth, rows, six_d), F32),
        compiler_params=_cparams("parallel", "parallel"),
        name="adaln_mod",
    )(cvec, w_ada, b_ada.reshape(depth, 1, six_d))


def _per_half(fn, mods, *arrays):
    half = arrays[0].shape[0] // 2
    return jnp.concatenate([fn(m, *(a[i * half:(i + 1) * half] for a in arrays))
                            for i, m in enumerate(mods)], axis=0)


def _inproj_kernel(x_ref, moda_ref, modb_ref, w_ref, cosa_ref, cosb_ref, saa_ref, sab_ref,
                   sba_ref, sbb_ref, qg_ref, kg_ref,
                   mqk_ref, mv_ref, mo_ref, gate_ref, dq_ref, dk_ref, dv_ref,
                   gq_ref, gk_ref, gv_ref):
    mods = (moda_ref[0, 0], modb_ref[0, 0])
    xm = _per_half(lambda m, x: x * (1.0 + m[1:2]) + m[0:1], mods, x_ref[0]).astype(BF16)
    cos = jnp.concatenate([cosa_ref[...], cosb_ref[...]], axis=0)
    sa = jnp.concatenate([saa_ref[...], sab_ref[...]], axis=0)
    sb = jnp.concatenate([sba_ref[...], sbb_ref[...]], axis=0)
    scale = ATTN_Q_SCALE

    def proj(lo, n):
        return _dot(xm, w_ref[:, lo:lo + n])

    def rope(a):
        return (a * cos + pltpu.roll(a, LANES - ROPE_HALF, 1) * sa
                + pltpu.roll(a, ROPE_HALF, 1) * sb)

    mqk_ref[0] = proj(C_MQK, 2 * MLSTM_W).astype(BF16)
    mv_ref[0] = proj(C_MV, MLSTM_W).astype(BF16)
    mo_ref[0] = proj(C_MO, MLSTM_W).astype(BF16)
    gate_ref[0] = proj(C_GATE, LANES)

    acc = proj(C_DQ, DIFF_W)
    for j in range(DIFF_W // LANES):
        dq_ref[0, :, j * LANES:(j + 1) * LANES] = (
            rope(acc[:, j * LANES:(j + 1) * LANES]) * scale).astype(BF16)
    acc = proj(C_DK, DIFF_W)
    for j in range(DIFF_W // LANES):
        dk_ref[0, :, j * LANES:(j + 1) * LANES] = rope(acc[:, j * LANES:(j + 1) * LANES]).astype(BF16)
    dv_ref[0] = proj(C_DV, DIFF_W).astype(BF16)

    head_mean = _head_mean_matrix(LANES)

    def rms(a, g):
        ms = _dot_split_lhs(a * a, head_mean)
        return a * lax.rsqrt(ms + LN_EPS) * g

    acc = proj(C_GQ, GQA_W)
    qg = qg_ref[...]
    for j in range(GQA_W // LANES):
        gq_ref[0, :, j * LANES:(j + 1) * LANES] = (
            rope(rms(acc[:, j * LANES:(j + 1) * LANES], qg)) * scale).astype(BF16)
    gk_ref[0] = rope(rms(proj(C_GK, GQA_KV_W), kg_ref[...])).astype(BF16)

    lane = lax.broadcasted_iota(jnp.int32, (1, 2 * LANES), 1)
    ones_col = jnp.logical_or(lane == HEAD_DIM, lane == LANES).astype(F32)
    gv_ref[0] = (proj(C_GV, 2 * LANES) + ones_col).astype(BF16)


def _inproj_call(xs, mod, w, tabs, qg, kg, n_lat):
    b, t, d = xs.shape
    flat = _FlatTiling(b, t, n_lat)
    row = lambda width: pl.BlockSpec((1, flat.tm, width), lambda i: (0, i, 0))
    tab = lambda h: pl.BlockSpec((ROW_TILE, LANES), lambda i: (flat.unit_in_batch(i, h), 0))
    vec = pl.BlockSpec((1, LANES), lambda i: (0, 0))
    widths = (2 * MLSTM_W, MLSTM_W, MLSTM_W, LANES, DIFF_W, DIFF_W, DIFF_W, GQA_W, GQA_KV_W, 2 * LANES)
    dtypes = (BF16, BF16, BF16, F32, BF16, BF16, BF16, BF16, BF16, BF16)
    outs = pl.pallas_call(
        _inproj_kernel,
        grid=(flat.n_tiles,),
        in_specs=[row(d), flat.mod_spec(0, d), flat.mod_spec(1, d),
                  _resident((d, IN_COLS_PADDED)),
                  tab(0), tab(1), tab(0), tab(1), tab(0), tab(1), vec, vec],
        out_specs=[row(wd) for wd in widths],
        out_shape=[jax.ShapeDtypeStruct((1, b * t, wd), dt) for wd, dt in zip(widths, dtypes)],
        compiler_params=_cparams("parallel"),
        name="in_proj",
    )(xs.reshape(1, b * t, d), mod, mod, w, tabs[0], tabs[0], tabs[1], tabs[1], tabs[2], tabs[2], qg, kg)
    return [o.reshape(b, t, -1) for o in outs]


def _mlstm_kernel(mqk_ref, mv_ref, mo_ref, gate_ref, cw_ref, cb_ref, gb_ref, ng_ref, out_ref,
                  qk_s, h_s, *, n_lat, n_tot):
    lc = ROW_TILE
    t_rows = n_tot * lc
    halo = BF16_SUBLANES

    cw = cw_ref[...]
    cb = cb_ref[...]
    lane_qk = lax.broadcasted_iota(jnp.int32, (1, 2 * MLSTM_W), 1)
    qscale = jnp.where(lane_qk < MLSTM_W, HEAD_DIM ** -0.5, 1.0)

    def conv_body(c, carry):
        r0 = pl.multiple_of(c * lc, lc)
        main = mqk_ref[0, pl.ds(r0, lc), :].astype(F32)
        ts = pl.multiple_of(jnp.maximum(r0 - halo, 0), halo)
        bs = pl.multiple_of(jnp.minimum(r0 + lc, t_rows - halo), halo)
        top_ok = jnp.logical_and(c != 0, c != n_lat).astype(F32)
        bot_ok = jnp.logical_and(c != n_lat - 1, c != n_tot - 1).astype(F32)
        top = mqk_ref[0, pl.ds(ts, halo), :].astype(F32) * top_ok
        bot = mqk_ref[0, pl.ds(bs, halo), :].astype(F32) * bot_ok
        win = jnp.concatenate([top, main, bot], axis=0)
        n = lc + 2 * halo
        y = win * cw[CONV_W // 2:CONV_W // 2 + 1]
        for j in range(CONV_W):
            if j != CONV_W // 2:
                y = y + pltpu.roll(win, (CONV_W // 2 - j) % n, 0) * cw[j:j + 1]
        y = y[halo:halo + lc] + cb
        qk_s[pl.ds(r0, lc), :] = (y * jax.nn.sigmoid(y) * qscale).astype(BF16)
        return carry

    lax.fori_loop(0, n_tot, conv_body, 0)

    row_i = lax.broadcasted_iota(jnp.int32, (lc, lc), 0)
    col_i = lax.broadcasted_iota(jnp.int32, (lc, lc), 1)
    masks_t = (row_i <= col_i, row_i >= col_i)
    tri_row = (masks_t[0].astype(BF16), masks_t[1].astype(BF16))
    tri_col = (tri_row[1], tri_row[0])
    sub = lax.broadcasted_iota(jnp.int32, (LANES, lc), 0)
    sels = (sub < HEAD_DIM, sub >= HEAD_DIM)
    one_rows = ((sub == HEAD_DIM).astype(BF16), (sub == 0).astype(BF16))
    den_row = (HEAD_DIM, 0)
    gb = gb_ref[...]
    h_s[...] = jnp.zeros_like(h_s)
    n_streams = 2 * MLSTM_HEADS

    def step(j, carry):
        cns, ms = carry
        new_cns, new_ms = list(cns), list(ms)
        for d in range(2):
            c = (j + n_lat) % n_tot if d == 0 else n_tot - 1 - j
            r0 = pl.multiple_of(c * lc, lc)
            gates = gate_ref[0, pl.ds(r0, lc), :] + gb
            gates_t = gates.T
            cum = _dot_split_rhs(tri_col[d], jax.nn.log_sigmoid(gates))
            cum_t = _dot_split_lhs(jax.nn.log_sigmoid(gates_t), tri_row[d])
            qk = qk_s[pl.ds(r0, lc), :]
            vv = mv_ref[0, pl.ds(r0, lc), :]
            last = lc - 1 if d == 0 else 0
            for p in range(MLSTM_HEADS // 2):
                q_t = qk[:, p * LANES:(p + 1) * LANES].T
                k128 = qk[:, MLSTM_W + p * LANES:MLSTM_W + (p + 1) * LANES]
                v_t = vv[:, p * LANES:(p + 1) * LANES].T
                halves = []
                for e in range(2):
                    hh = 2 * p + e
                    sidx = d * MLSTM_HEADS + hh
                    ci = d * MLSTM_HEADS + hh
                    cf = 2 * MLSTM_HEADS + d * MLSTM_HEADS + hh
                    qm_t = jnp.where(sels[e], q_t, jnp.zeros_like(q_t))
                    v1_t = jnp.where(sels[e], v_t, one_rows[e])
                    i_row = gates_t[ci:ci + 1, :]
                    a_row = cum_t[cf:cf + 1, :]
                    r_col = gates[:, ci:ci + 1] - cum[:, cf:cf + 1]
                    m = ms[sidx]
                    cn_t = cns[sidx]
                    dm = jnp.where(masks_t[d], a_row + r_col, -jnp.inf)
                    inter = a_row + m
                    m_t = jnp.maximum(jnp.max(dm, axis=0, keepdims=True), inter)
                    w_prev = jnp.exp(inter - m_t)
                    sw = (_dot(k128, qm_t) * jnp.exp(dm - m_t)).astype(BF16)
                    nd = _dot(v1_t, sw) + w_prev * _dot(cn_t.astype(BF16), qm_t)
                    den = nd[den_row[e]:den_row[e] + 1, :]
                    halves.append(nd * (1.0 / jnp.maximum(jnp.abs(den), jnp.exp(-m_t))))
                    b_last = cum_t[cf:cf + 1, last:last + 1]
                    logu = b_last - a_row + i_row
                    m_new = jnp.maximum(b_last + m, jnp.max(logu, axis=1, keepdims=True))
                    vu = (v1_t.astype(F32) * jnp.exp(logu - m_new)).astype(BF16)
                    new_cns[sidx] = jnp.exp(b_last + m - m_new) * cn_t + _dot(vu, k128)
                    new_ms[sidx] = m_new
                h_pair = jnp.where(sels[0], halves[0], halves[1]).T
                h_s[pl.ds(r0, lc), p * LANES:(p + 1) * LANES] += h_pair
        return tuple(new_cns), tuple(new_ms)

    init = (tuple(jnp.zeros((LANES, LANES), F32) for _ in range(n_streams)),
            tuple(jnp.zeros((1, 1), F32) for _ in range(n_streams)))
    lax.fori_loop(0, n_tot, step, init, unroll=2)

    head_mean = _head_mean_matrix(MLSTM_W)
    ng = ng_ref[...]

    def out_body(c, carry):
        r0 = pl.multiple_of(c * lc, lc)
        hb = h_s[pl.ds(r0, lc), :]
        hc = hb - _dot_split_lhs(hb, head_mean)
        var = _dot_split_lhs(hc * hc, head_mean)
        o = mo_ref[0, pl.ds(r0, lc), :].astype(F32)
        out_ref[0, pl.ds(r0, lc), :] = (hc * lax.rsqrt(var + LN_EPS) * ng * jax.nn.sigmoid(o)).astype(BF16)
        return carry

    lax.fori_loop(0, n_tot, out_body, 0, unroll=2)


def _mlstm_call(mqk, mv, mo, gate, cw, cb, gb, ng, n_lat):
    b, t, _ = mqk.shape
    n_tot = t // ROW_TILE
    seq = lambda width: pl.BlockSpec((1, t, width), lambda bi: (bi, 0, 0))
    full = lambda a: pl.BlockSpec(a.shape, lambda bi: (0, 0))
    return pl.pallas_call(
        functools.partial(_mlstm_kernel, n_lat=n_lat, n_tot=n_tot),
        grid=(b,),
        in_specs=[seq(2 * MLSTM_W), seq(MLSTM_W), seq(MLSTM_W), seq(LANES),
                  full(cw), full(cb), full(gb), full(ng)],
        out_specs=seq(MLSTM_W),
        out_shape=jax.ShapeDtypeStruct((b, t, MLSTM_W), BF16),
        scratch_shapes=[pltpu.VMEM((t, 2 * MLSTM_W), BF16), pltpu.VMEM((t, MLSTM_W), F32)],
        compiler_params=_cparams("parallel"),
        name="mlstm",
    )(mqk, mv, mo, gate, cw, cb, gb, ng)


def _key_blocks(first, n_rows, tk):
    blocks, r = [], first
    while r < first + n_rows:
        size = min(tk, first + n_rows - r)
        blocks.append((r, size))
        r += size
    return blocks


def _attn_pair_tile(qms, k_ref, v1_fn, blocks):
    ms, accs = [None, None], [None, None]
    for start, size in blocks:
        k = k_ref[0, pl.ds(start, size), :]
        for e in range(2):
            s = _dot_nt(qms[e], k)
            row_max = jnp.max(s, axis=1, keepdims=True)
            m_new = row_max if ms[e] is None else jnp.maximum(ms[e], row_max)
            pv = _dot(jnp.exp2(s - m_new).astype(BF16), v1_fn(e, start, size))
            accs[e] = pv if accs[e] is None else jnp.exp2(ms[e] - m_new) * accs[e] + pv
            ms[e] = m_new
    return accs


def _for_query_tiles(n_lat_rows, n_ctx_rows, tiles, tile_fn):
    tq, tk, unroll, fold_tail = tiles
    tq = min(tq, n_lat_rows)
    lat_blocks = _key_blocks(0, n_lat_rows, tk) + _key_blocks(n_lat_rows, n_ctx_rows, tk)
    if fold_tail and len(lat_blocks) > 1:
        (start, size), (_, tail) = lat_blocks[-2:]
        lat_blocks = lat_blocks[:-2] + [(start, size + tail)]
    ctx_blocks = _key_blocks(n_lat_rows, n_ctx_rows, tk)

    def lat_body(i, carry):
        tile_fn(pl.multiple_of(i * tq, tq), tq, lat_blocks)
        return carry

    lax.fori_loop(0, n_lat_rows // tq, lat_body, 0, unroll=unroll)
    tq_ctx = min(tq, n_ctx_rows)
    for i in range(n_ctx_rows // tq_ctx):
        tile_fn(n_lat_rows + i * tq_ctx, tq_ctx, ctx_blocks)


def _diff_attn_kernel(q_ref, k_ref, v_ref, lam_ref, g_ref, o_ref, *, n_lat_rows, tiles, lam_init):
    dv = 2 * HEAD_DIM
    n_ctx_rows = q_ref.shape[1] - n_lat_rows
    lane = lax.broadcasted_iota(jnp.int32, (1, LANES), 1)
    ones_row = (lane == 0).astype(BF16)
    lv = lam_ref[0]
    lam = (jnp.exp(jnp.sum(lv[0:1] * lv[1:2], axis=1, keepdims=True))
           - jnp.exp(jnp.sum(lv[2:3] * lv[3:4], axis=1, keepdims=True)) + lam_init)
    gain = g_ref[...] * (1.0 - lam_init)

    def v1_fn(e, start, size):
        ones_blk = jnp.broadcast_to(ones_row, (size, LANES))
        return jnp.concatenate([v_ref[0, pl.ds(start, size), :], ones_blk], axis=1)

    def tile_fn(r0, rows, blocks):
        q = q_ref[0, pl.ds(r0, rows), :]
        zero = jnp.zeros_like(q)
        qms = (jnp.where(lane < HEAD_DIM, q, zero), jnp.where(lane >= HEAD_DIM, q, zero))
        a1, a2 = _attn_pair_tile(qms, k_ref, v1_fn, blocks)
        o = a1[:, :dv] * (1.0 / a1[:, dv:dv + 1]) - lam * (a2[:, :dv] * (1.0 / a2[:, dv:dv + 1]))
        ms = jnp.mean(o * o, axis=1, keepdims=True)
        o_ref[0, pl.ds(r0, rows), :] = (o * lax.rsqrt(ms + LN_EPS) * gain).astype(BF16)

    _for_query_tiles(n_lat_rows, n_ctx_rows, tiles, tile_fn)


def _diff_attn_call(dq, dk, dv, lam_vecs, g, n_lat, lam_init):
    b, t, _ = dq.shape
    seq = pl.BlockSpec((1, t, LANES), lambda bi, h: (bi, 0, h))
    return pl.pallas_call(
        functools.partial(_diff_attn_kernel, n_lat_rows=n_lat * ROW_TILE, tiles=DIFF_TILES,
                          lam_init=lam_init),
        grid=(b, DIFF_HEADS),
        in_specs=[seq, seq, seq,
                  pl.BlockSpec((1, 4, HEAD_DIM), lambda bi, h: (0, 0, 0)),
                  pl.BlockSpec((1, LANES), lambda bi, h: (0, 0))],
        out_specs=seq,
        out_shape=jax.ShapeDtypeStruct((b, t, DIFF_W), BF16),
        compiler_params=_cparams("parallel", "parallel"),
        name="diff_attn",
    )(dq, dk, dv, lam_vecs, g)


def _gqa_attn_kernel(q_ref, k_ref, v_ref, o_ref, *, n_lat_rows, tiles):
    n_ctx_rows = q_ref.shape[1] - n_lat_rows
    lane = lax.broadcasted_iota(jnp.int32, (1, LANES), 1)
    first_half = lane < HEAD_DIM
    den_lane = (HEAD_DIM, 0)

    def v1_fn(e, start, size):
        return v_ref[0, pl.ds(start, size), e * LANES:(e + 1) * LANES]

    def tile_fn(r0, rows, blocks):
        q = q_ref[0, pl.ds(r0, rows), :]
        zero = jnp.zeros_like(q)
        qms = (jnp.where(first_half, q, zero), jnp.where(first_half, zero, q))
        accs = _attn_pair_tile(qms, k_ref, v1_fn, blocks)
        halves = [acc * (1.0 / acc[:, den_lane[e]:den_lane[e] + 1]) for e, acc in enumerate(accs)]
        o_ref[0, pl.ds(r0, rows), :] = jnp.where(first_half, halves[0], halves[1]).astype(BF16)

    _for_query_tiles(n_lat_rows, n_ctx_rows, tiles, tile_fn)


def _gqa_attn_call(gq, gk, gv, n_lat):
    b, t, _ = gq.shape
    pair = pl.BlockSpec((1, t, LANES), lambda bi, p: (bi, 0, p))
    return pl.pallas_call(
        functools.partial(_gqa_attn_kernel, n_lat_rows=n_lat * ROW_TILE, tiles=GQA_TILES),
        grid=(b, GQA_W // LANES),
        in_specs=[pair,
                  pl.BlockSpec((1, t, GQA_KV_W), lambda bi, p: (bi, 0, 0)),
                  pl.BlockSpec((1, t, 2 * LANES), lambda bi, p: (bi, 0, 0))],
        out_specs=pair,
        out_shape=jax.ShapeDtypeStruct((b, t, GQA_W), BF16),
        compiler_params=_cparams("parallel", "arbitrary"),
        name="gqa_attn",
    )(gq, gk, gv)


def _post_kernel(a_ref, bd_ref, cg_ref, x_ref, moda_ref, modb_ref, w_ref, g1_ref, b1_ref,
                 wi_ref, wo_ref, g2_ref, b2_ref, o_ref, *, alpha, d_ff, chunk):
    mods = (moda_ref[0, 0], modb_ref[0, 0])
    y = (_dot(a_ref[0], w_ref[0:MLSTM_W, :])
         + _dot(bd_ref[0], w_ref[MLSTM_W:MLSTM_W + DIFF_W, :])
         + _dot(cg_ref[0], w_ref[MLSTM_W + DIFF_W:, :]))
    z = _per_half(lambda m, xh, yh: alpha * xh + m[2:3] * yh, mods, x_ref[0], y)
    x = _layer_norm(z, g1_ref[...], b1_ref[...])
    xm = _per_half(lambda m, xh: xh * (1.0 + m[4:5]) + m[3:4], mods, x).astype(BF16)
    acc = jnp.zeros(x.shape, F32)
    for c in range(d_ff // chunk):
        gate = _dot(xm, wi_ref[:, c * chunk:(c + 1) * chunk])
        up = _dot(xm, wi_ref[:, d_ff + c * chunk:d_ff + (c + 1) * chunk])
        act = (gate * jax.nn.sigmoid(gate) * up).astype(BF16)
        acc = acc + _dot(act, wo_ref[c * chunk:(c + 1) * chunk, :])
    z = _per_half(lambda m, xh, ah: alpha * xh + m[5:6] * ah, mods, x, acc)
    o_ref[0] = _layer_norm(z, g2_ref[...], b2_ref[...])


def _post_call(a, bd, cg, xs, mod, w, g1, b1, wi, wo, g2, b2, n_lat, latent_only, alpha):
    b, t, d = xs.shape
    d_ff = wo.shape[0]
    tm = LINEAR_TILE
    full = lambda arr: _resident(arr.shape)
    if latent_only:
        s = n_lat * ROW_TILE
        assert s % tm == 0
        grid = (b, s // tm)
        row = lambda width: pl.BlockSpec((1, tm, width), lambda bi, i: (bi, i, 0))
        mod_specs = [pl.BlockSpec((1, 1, 6, d), lambda bi, i: (bi, 0, 0, 0))] * 2
        vec = pl.BlockSpec((1, d), lambda bi, i: (0, 0))
        out_shape = (b, s, d)
        sem = ("parallel", "parallel")
    else:
        flat = _FlatTiling(b, t, n_lat)
        grid = (flat.n_tiles,)
        row = lambda width: pl.BlockSpec((1, tm, width), lambda i: (0, i, 0))
        mod_specs = [flat.mod_spec(0, d), flat.mod_spec(1, d)]
        vec = pl.BlockSpec((1, d), lambda i: (0, 0))
        out_shape = (1, b * t, d)
        sem = ("parallel",)
        a, bd, cg, xs = (v.reshape(1, b * t, -1) for v in (a, bd, cg, xs))
    out = pl.pallas_call(
        functools.partial(_post_kernel, alpha=alpha, d_ff=d_ff, chunk=2 * LANES),
        grid=grid,
        in_specs=[row(MLSTM_W), row(DIFF_W), row(GQA_W), row(d), *mod_specs,
                  full(w), vec, vec, full(wi), full(wo), vec, vec],
        out_specs=row(d),
        out_shape=jax.ShapeDtypeStruct(out_shape, F32),
        compiler_params=_cparams(*sem),
        name="out_proj_ffn",
    )(a, bd, cg, xs, mod, mod, w, g1, b1, wi, wo, g2, b2)
    return out if latent_only else out.reshape(b, t, d)


def _prep_w_in(w):
    zeros = lambda n: jnp.zeros((w.shape[0], n), w.dtype)
    o_gate = 4 * MLSTM_W
    o_dq = o_gate + 4 * MLSTM_HEADS
    o_gq = o_dq + 3 * DIFF_W
    o_gk = o_gq + GQA_W
    o_gv = o_gk + GQA_KV_W
    gq = w[:, o_gq:o_gk].reshape(-1, 2, 2, HEAD_DIM).transpose(0, 2, 1, 3).reshape(-1, GQA_W)
    gv = w[:, o_gv:o_gv + GQA_KV_W]
    cols = [w[:, :o_gate], w[:, o_dq:o_gq], gq, w[:, o_gk:o_gv],
            gv[:, :HEAD_DIM], zeros(LANES), gv[:, HEAD_DIM:],
            w[:, o_gate:o_dq], zeros(LANES - 4 * MLSTM_HEADS)]
    out = jnp.concatenate(cols, axis=1)
    assert out.shape[1] == IN_COLS_PADDED
    return out.astype(BF16)


def _prep_w_out(w):
    o_c = MLSTM_W + DIFF_W
    gq = w[o_c:].reshape(2, 2, HEAD_DIM, -1).transpose(1, 0, 2, 3).reshape(GQA_W, -1)
    return jnp.concatenate([w[:o_c], gq], axis=0).astype(BF16)


def _rope_tables(s, n_ctx_rows):
    pos = jnp.arange(s, dtype=jnp.int32)
    row = (pos // GRID_W).astype(F32)
    col = (pos % GRID_W).astype(F32)
    n_freq = HEAD_DIM // 4
    inv = ROPE_THETA ** (-jnp.arange(n_freq, dtype=F32) / n_freq)
    ar = row[:, None] * inv
    ac = col[:, None] * inv
    ang = jnp.concatenate([ar, ar, ac, ac], axis=-1)
    ang = jnp.concatenate([ang, ang], axis=-1)
    first = (jnp.arange(LANES) % (2 * ROPE_HALF)) < ROPE_HALF
    cos, sin = jnp.cos(ang), jnp.sin(ang)
    sa = jnp.where(first, -sin, 0.0)
    sb = jnp.where(first, 0.0, sin)
    pad = lambda a, v: jnp.concatenate([a, jnp.full((n_ctx_rows, LANES), v, F32)], axis=0)
    return pad(cos, 1.0), pad(sa, 0.0), pad(sb, 0.0)


def kernel(x, c, ctx, c_ctx, w_ada, b_ada, w_in, mlstm_conv_w, mlstm_conv_b, mlstm_gate_b, mlstm_norm_g, diff_lambda, diff_norm_g, gqa_q_norm_g, gqa_k_norm_g, w_out, ln1_g, ln1_b, w_ffn_in, w_ffn_out, ln2_g, ln2_b):
    b, s, d = x.shape
    n_ctx_rows = ctx.shape[1]
    depth = w_in.shape[0]
    assert s % ROW_TILE == 0 and n_ctx_rows % ROW_TILE == 0 and s % GRID_W == 0
    n_lat = s // ROW_TILE
    t = s + n_ctx_rows
    alpha = (2 * depth) ** 0.25

    ada_rows = -(-(b + 1) // 8) * 8
    cvec = jnp.concatenate([c, c_ctx[None, :], jnp.zeros((ada_rows - b - 1, d), F32)], axis=0)
    mod_all = _ada_call(cvec, w_ada, b_ada)
    mod_lat = mod_all[:, :b].reshape(depth, b, 1, 6, d)
    mod_ctx = jnp.broadcast_to(mod_all[:, b].reshape(depth, 1, 1, 6, d), (depth, b, 1, 6, d))
    mod_all = jnp.concatenate([mod_lat, mod_ctx], axis=2)

    tabs = _rope_tables(s, n_ctx_rows)
    tile2 = lambda v, n: jnp.tile(v, n).reshape(1, -1)
    xs = jnp.concatenate([x, ctx], axis=1)

    for l in range(depth):
        last = l == depth - 1
        lam_init = 0.8 - 0.6 * math.exp(-0.3 * l)
        mod = mod_all[l]
        (mqk, mv, mo, gate, dq, dk, dv, gq, gk, gv) = _inproj_call(
            xs, mod, _prep_w_in(w_in[l]), tabs,
            tile2(gqa_q_norm_g[l], LANES // HEAD_DIM), tile2(gqa_k_norm_g[l], LANES // HEAD_DIM), n_lat)
        cw = jnp.concatenate([mlstm_conv_w[l], jnp.zeros((8 - CONV_W, 2 * MLSTM_W), F32)], axis=0)
        gb = jnp.concatenate([mlstm_gate_b[l], jnp.zeros((LANES - 4 * MLSTM_HEADS,), F32)]).reshape(1, LANES)
        a = _mlstm_call(mqk, mv, mo, gate, cw, mlstm_conv_b[l].reshape(1, -1), gb,
                        mlstm_norm_g[l].reshape(1, -1), n_lat)
        bd = _diff_attn_call(dq, dk, dv, diff_lambda[l][None], diff_norm_g[l].reshape(1, -1), n_lat, lam_init)
        cg = _gqa_attn_call(gq, gk, gv, n_lat)
        xs = _post_call(a, bd, cg, xs, mod, _prep_w_out(w_out[l]),
                        ln1_g[l].reshape(1, -1), ln1_b[l].reshape(1, -1),
                        w_ffn_in[l].astype(BF16), w_ffn_out[l].astype(BF16),
                        ln2_g[l].reshape(1, -1), ln2_b[l].reshape(1, -1), n_lat, last, alpha)
    return xs
```
